```python
import functools
import jax, jax.numpy as jnp
from jax import lax
import numpy as np


D_MODEL = 1024
BATCH = 4
SEQ = 4096
DEPTH = 2
DEC_BATCH = 32
DEC_SEQ = 4
PAST_LEN = 16384
PAGE_SIZE = 128

CHUNK = 128
D_A = D_MODEL // 2
G_A = 4
CG_A = D_A // G_A
H_B = 8
HD_B = 64
D_B = H_B * HD_B
QBLK = 128
D_FF = 4 * D_MODEL
ALPHA = (2 * DEPTH) ** 0.25
BETA = (8 * DEPTH) ** -0.25
LN_EPS = 1e-5
SB_BIAS_INIT = -6.0
D_IN = 2 * D_A + 3 * D_B + 2 * D_MODEL
IN_SPLITS = (D_A, 2 * D_A, 2 * D_A + D_B, 2 * D_A + 2 * D_B, 2 * D_A + 3 * D_B,
             2 * D_A + 3 * D_B + D_MODEL)

kernel_name = 'gated_gmlp_stickbreak_decoder_step'


def layer_norm(x, g, b):
    xf = x.astype(jnp.float32)
    mu = jnp.mean(xf, axis=-1, keepdims=True)
    var = jnp.mean(jnp.square(xf - mu), axis=-1, keepdims=True)
    return ((xf - mu) * lax.rsqrt(var + LN_EPS) * g + b).astype(x.dtype)


def gmlp_spatial(v, w_s, b_s):
    B, T, _ = v.shape
    tc = min(T, CHUNK)
    w = jnp.tril(w_s)[:, :tc, :tc]
    vb = v.reshape(B, T // tc, tc, G_A, CG_A)
    mixed = jnp.einsum('gts,bnsgc->bntgc', w, vb) + b_s[:, :tc].T[None, None, :, :, None]
    return mixed.reshape(B, T, D_A)


def sb_weights(z, q_pos, k_pos):
    mask = k_pos[None, :] < q_pos[:, None]
    log_beta = jax.nn.log_sigmoid(z)
    log_1m = jnp.where(mask, log_beta - z, 0.0)
    excl = lax.cumsum(log_1m, axis=3, reverse=True) - log_1m
    return jnp.where(mask, jnp.exp(log_beta + excl), 0.0)


def sb_attn_prompt(q, k, v, bias):
    B, S, H, HD = q.shape
    nb = S // QBLK
    qf = (q.astype(jnp.float32) * HD ** -0.5).reshape(B, nb, QBLK, H, HD).transpose(1, 0, 2, 3, 4)
    kf = k.astype(jnp.float32)
    bf = bias.astype(jnp.float32)[None, :, None, None]
    k_pos = jnp.arange(S)

    def one(args):
        qi, i = args
        z = jnp.einsum('bthd,bshd->bhts', qi, kf) + bf
        a = sb_weights(z, i * QBLK + jnp.arange(QBLK), k_pos)
        return jnp.einsum('bhts,bshd->bthd', a.astype(v.dtype), v)

    o = lax.map(one, (qf, jnp.arange(nb)))
    return o.transpose(1, 0, 2, 3, 4).reshape(B, S, H, HD)


def sb_attn_sample(q, k, v, bias, k_past, v_past):
    B, T, H, HD = q.shape
    P = k_past.shape[1]
    qf = q.astype(jnp.float32) * HD ** -0.5
    z = jnp.concatenate([jnp.einsum('bthd,bshd->bhts', qf, k_past.astype(jnp.float32)),
                         jnp.einsum('bthd,bshd->bhts', qf, k.astype(jnp.float32))], axis=-1)
    z = z + bias.astype(jnp.float32)[None, :, None, None]
    a = sb_weights(z, P + jnp.arange(T), jnp.arange(P + T))
    return (jnp.einsum('bhts,bshd->bthd', a[..., :P].astype(v.dtype), v_past)
            + jnp.einsum('bhts,bshd->bthd', a[..., P:].astype(v.dtype), v))


def trunk_layer(x, c, attn, w_ada, b_ada, w_in, b_sb, ln_v_g, ln_v_b, w_s, b_s, w_a, w_b, w_o,
                ln1_g, ln1_b, w_up, b_up, w_down, b_down, ln2_g, ln2_b):
    B, T, _ = x.shape
    mods = jax.nn.silu(c) @ w_ada + b_ada
    sh1, sc1, g1, sh2, sc2, g2 = [m[:, None, :] for m in jnp.split(mods, 6, axis=-1)]
    h = x * (1.0 + sc1) + sh1
    p = h @ w_in
    pu, pv, pq, pk, pvb, pga, pgb = jnp.split(p, IN_SPLITS, axis=-1)
    u = jax.nn.gelu(pu, approximate=False)
    vn = layer_norm(jax.nn.gelu(pv, approximate=False), ln_v_g, ln_v_b)
    y_a = (u * gmlp_spatial(vn, w_s, b_s)) @ w_a
    q = pq.reshape(B, T, H_B, HD_B)
    k = pk.reshape(B, T, H_B, HD_B)
    v = pvb.reshape(B, T, H_B, HD_B)
    y_b = attn(q, k, v, b_sb).reshape(B, T, D_B) @ w_b
    t = (jax.nn.sigmoid(pga) * y_a + jax.nn.sigmoid(pgb) * y_b) @ w_o
    x = layer_norm(ALPHA * x + g1 * t, ln1_g, ln1_b)
    h2 = x * (1.0 + sc2) + sh2
    f = jnp.square(jax.nn.relu(h2 @ w_up + b_up)) @ w_down + b_down
    x = layer_norm(ALPHA * x + g2 * f, ln2_g, ln2_b)
    return x, k, v, vn


def setup_inputs(seed: int = 0) -> dict:
    key = jax.random.key(seed)
    ks = jax.random.split(key, 32)
    n_pages = PAST_LEN // PAGE_SIZE
    n_used = DEC_BATCH * n_pages
    n_pool = n_used + n_used // 4
    nrm = lambda k, shape, s: jax.random.normal(k, shape, jnp.float32) * s
    page_table = jax.random.permutation(ks[0], n_pool)[:n_used].reshape(DEC_BATCH, n_pages).astype(jnp.int32)
    return {
        'x_prompt': nrm(ks[1], (BATCH, SEQ, D_MODEL), 1.0),
        'x_sample': nrm(ks[2], (DEC_BATCH, DEC_SEQ, D_MODEL), 1.0),
        'cache_k': nrm(ks[3], (DEPTH, n_pool, PAGE_SIZE, H_B, HD_B), 1.0),
        'cache_v': nrm(ks[4], (DEPTH, n_pool, PAGE_SIZE, H_B, HD_B), 1.0),
        'page_table': page_table,
        'c_prompt': nrm(ks[5], (BATCH, D_MODEL), 1.0),
        'c_sample': nrm(ks[6], (DEC_BATCH, D_MODEL), 1.0),
        'w_ada': nrm(ks[7], (DEPTH, D_MODEL, 6 * D_MODEL), 0.5 * D_MODEL ** -0.5),
        'b_ada': nrm(ks[8], (DEPTH, 6 * D_MODEL), 0.01),
        'w_in': nrm(ks[9], (DEPTH, D_MODEL, D_IN), D_MODEL ** -0.5),
        'b_sb': SB_BIAS_INIT + nrm(ks[25], (DEPTH, H_B), 0.1),
        'ln_v_g': 1.0 + nrm(ks[10], (DEPTH, D_A), 0.01),
        'ln_v_b': nrm(ks[11], (DEPTH, D_A), 0.01),
        'w_s': nrm(ks[12], (DEPTH, G_A, CHUNK, CHUNK), CHUNK ** -0.5),
        'b_s': 1.0 + nrm(ks[13], (DEPTH, G_A, CHUNK), 0.01),
        'w_a': nrm(ks[14], (DEPTH, D_A, D_MODEL), BETA * D_A ** -0.5),
        'w_b': nrm(ks[15], (DEPTH, D_B, D_MODEL), BETA * D_B ** -0.5),
        'w_o': nrm(ks[16], (DEPTH, D_MODEL, D_MODEL), BETA * D_MODEL ** -0.5),
        'ln1_g': 1.0 + nrm(ks[17], (DEPTH, D_MODEL), 0.01),
        'ln1_b': nrm(ks[18], (DEPTH, D_MODEL), 0.01),
        'w_up': nrm(ks[19], (DEPTH, D_MODEL, D_FF), D_MODEL ** -0.5),
        'b_up': nrm(ks[20], (DEPTH, D_FF), 0.01),
        'w_down': nrm(ks[21], (DEPTH, D_FF, D_MODEL), BETA * D_FF ** -0.5),
        'b_down': nrm(ks[22], (DEPTH, D_MODEL), 0.01),
        'ln2_g': 1.0 + nrm(ks[23], (DEPTH, D_MODEL), 0.01),
        'ln2_b': nrm(ks[24], (DEPTH, D_MODEL), 0.01),
    }


def reference(x_prompt, x_sample, cache_k, cache_v, page_table, c_prompt, c_sample,
              w_ada, b_ada, w_in, b_sb, ln_v_g, ln_v_b, w_s, b_s, w_a, w_b, w_o,
              ln1_g, ln1_b, w_up, b_up, w_down, b_down, ln2_g, ln2_b):
    n_seq, n_pages = page_table.shape
    past = n_pages * cache_k.shape[2]
    yp, ys = x_prompt, x_sample
    kp_l, vp_l, gp_l, ks_l, vs_l, gs_l = [], [], [], [], [], []
    for l in range(DEPTH):
        lw = (w_ada[l], b_ada[l], w_in[l], b_sb[l], ln_v_g[l], ln_v_b[l], w_s[l], b_s[l], w_a[l],
              w_b[l], w_o[l], ln1_g[l], ln1_b[l], w_up[l], b_up[l], w_down[l], b_down[l],
              ln2_g[l], ln2_b[l])
        yp, kp, vp, gp = trunk_layer(yp, c_prompt, sb_attn_prompt, *lw)
        k_past = cache_k[l][page_table].reshape(n_seq, past, H_B, HD_B)
        v_past = cache_v[l][page_table].reshape(n_seq, past, H_B, HD_B)
        attn_s = functools.partial(sb_attn_sample, k_past=k_past, v_past=v_past)
        ys, ksm, vsm, gsm = trunk_layer(ys, c_sample, attn_s, *lw)
        kp_l.append(kp); vp_l.append(vp); gp_l.append(gp[:, -CHUNK:])
        ks_l.append(ksm); vs_l.append(vsm); gs_l.append(gsm)
    return (yp, ys, jnp.stack(kp_l), jnp.stack(vp_l), jnp.stack(gp_l),
            jnp.stack(ks_l), jnp.stack(vs_l), jnp.stack(gs_l))
```

```python
import functools

import jax
import jax.numpy as jnp
from jax import lax
from jax.experimental import pallas as pl
from jax.experimental.pallas import tpu as pltpu

F32 = jnp.float32
BF16 = jnp.bfloat16

CHUNK = 128
G_A = 4
H_B = 8
HD_B = 64
LN_EPS = 1e-5
LANES = 128
ATT_TQ = 256
ATT_TK = 256
PAGES_PER_STEP = 8
VMEM_LIMIT = 56 * 1024 * 1024


def _ln(x, g, b):
    mu = jnp.mean(x, axis=-1, keepdims=True)
    xc = x - mu
    var = jnp.mean(xc * xc, axis=-1, keepdims=True)
    return xc * lax.rsqrt(var + LN_EPS) * g + b


def _gelu(x):
    return 0.5 * x * (1.0 + lax.erf(x * (2.0 ** -0.5)))


def _sigmoid(x):
    return 1.0 / (1.0 + jnp.exp(-x))


def _div_pow2(x, n):
    assert n & (n - 1) == 0
    return x >> (n.bit_length() - 1)


def _softplus(z):
    return jnp.maximum(z, 0.0) + jnp.log(1.0 + jnp.exp(-jnp.abs(z)))


def _mods_kernel(c_ref, w_ref, b_ref, o_ref):
    c = c_ref[...]
    s = c * _sigmoid(c)
    o_ref[...] = jnp.dot(s, w_ref[...], preferred_element_type=F32,
                         precision=lax.Precision.HIGHEST) + b_ref[...]


def _mods_call(c_all, w_ada, b_ada):
    depth, d, n6 = w_ada.shape
    rows = c_all.shape[0]
    tn = 1536
    return pl.pallas_call(
        _mods_kernel,
        grid=(depth, n6 // tn),
        in_specs=[
            pl.BlockSpec((rows, d), lambda l, j: (0, 0)),
            pl.BlockSpec((None, d, tn), lambda l, j: (l, 0, j)),
            pl.BlockSpec((None, 1, tn), lambda l, j: (l, 0, j)),
        ],
        out_specs=pl.BlockSpec((None, rows, tn), lambda l, j: (l, 0, j)),
        out_shape=jax.ShapeDtypeStruct((depth, rows, n6), F32),
        compiler_params=pltpu.CompilerParams(
            dimension_semantics=("arbitrary", "arbitrary"), vmem_limit_bytes=VMEM_LIMIT),
    )(c_all, w_ada, b_ada.reshape(depth, 1, n6))


def _mod_spec(per_row, tm, d, tiles_per_batch, k):
    if per_row:
        return pl.BlockSpec((tm, d), lambda i: (i, k))
    return pl.BlockSpec((None, None, 1, d), lambda i: (i // tiles_per_batch, k, 0, 0))


def _const_spec(shape):
    nd = len(shape)
    return pl.BlockSpec(shape, lambda i: (0,) * nd)


def _inproj_kernel(x_ref, sh_ref, sc_ref, w_ref, wkvt_ref, lng_ref, lnb_ref,
                   u_ref, vn_ref, q_ref, k_ref, v_ref, ga_ref, gb_ref, *, d_a, d_b, d_model, kv_t):
    h = (x_ref[...] * (1.0 + sc_ref[...]) + sh_ref[...]).astype(BF16)

    def seg(a, n):
        return jnp.dot(h, w_ref[:, a:a + n], preferred_element_type=F32)

    def seg_t(a, n):
        return lax.dot_general(wkvt_ref[a:a + n, :], h, (((1,), (1,)), ((), ())),
                               preferred_element_type=F32)

    o = 0
    u_ref[...] = _gelu(seg(o, d_a)).astype(u_ref.dtype)
    o += d_a
    vn_ref[...] = _ln(_gelu(seg(o, d_a)), lng_ref[...], lnb_ref[...])
    o += d_a
    q_ref[...] = (seg(o, d_b) * (HD_B ** -0.5)).astype(q_ref.dtype)
    o += d_b
    if kv_t:
        k_ref[...] = seg_t(0, d_b)
        v_ref[...] = seg_t(d_b, d_b)
    else:
        k_ref[...] = seg(o, d_b)
        v_ref[...] = seg(o + d_b, d_b)
    o += 2 * d_b
    ga_ref[...] = _sigmoid(seg(o, d_model)).astype(ga_ref.dtype)
    o += d_model
    gb_ref[...] = _sigmoid(seg(o, d_model)).astype(gb_ref.dtype)


def _inproj_call(x2, mods, per_row, tiles_per_batch, tm, w_in, w_kvt, ln_g, ln_b, d_a, d_b, kv_t):
    n, d = x2.shape
    d_in = w_in.shape[1]
    row = lambda c: pl.BlockSpec((tm, c), lambda i: (i, 0))
    if kv_t:
        tpb = tiles_per_batch
        kv_shape = jax.ShapeDtypeStruct((n // (tpb * tm), d_b, tpb * tm), F32)
        kv_spec = pl.BlockSpec((None, d_b, tm), lambda i: (i // tpb, 0, i % tpb))
    else:
        kv_shape = jax.ShapeDtypeStruct((n, d_b), F32)
        kv_spec = row(d_b)
    out_shape = (
        jax.ShapeDtypeStruct((n, d_a), BF16),
        jax.ShapeDtypeStruct((n, d_a), F32),
        jax.ShapeDtypeStruct((n, d_b), BF16),
        kv_shape,
        kv_shape,
        jax.ShapeDtypeStruct((n, d), BF16),
        jax.ShapeDtypeStruct((n, d), BF16),
    )
    return pl.pallas_call(
        functools.partial(_inproj_kernel, d_a=d_a, d_b=d_b, d_model=d, kv_t=kv_t),
        grid=(n // tm,),
        in_specs=[
            row(d),
            _mod_spec(per_row, tm, d, tiles_per_batch, 0),
            _mod_spec(per_row, tm, d, tiles_per_batch, 1),
            _const_spec((d, d_in)),
            _const_spec(w_kvt.shape),
            _const_spec((1, d_a)),
            _const_spec((1, d_a)),
        ],
        out_specs=(row(d_a), row(d_a), row(d_b), kv_spec, kv_spec, row(d), row(d)),
        out_shape=out_shape,
        compiler_params=pltpu.CompilerParams(
            dimension_semantics=("arbitrary",), vmem_limit_bytes=VMEM_LIMIT),
    )(x2, mods, mods, w_in, w_kvt, ln_g.reshape(1, d_a), ln_b.reshape(1, d_a))


def _sb_tile(z, ls, carry, tri):
    sp = _softplus(z)
    if tri is not None:
        sp = jnp.where(tri, sp, 0.0)
    excl = jnp.dot(sp.astype(BF16), ls, preferred_element_type=F32)
    a = jnp.exp(z - sp - excl - carry)
    if tri is not None:
        a = jnp.where(tri, a, 0.0)
    return a, carry + excl[:, :1] + sp[:, :1]


def _attn_prompt_kernel(qi_ref, kj_ref, bias_ref, q_ref, k_ref, v_ref, ls_ref, o_ref,
                        carry_scr, acc_scr):
    n = pl.program_id(1)
    i = qi_ref[n]
    j = kj_ref[n]
    tq, tk = q_ref.shape[0], k_ref.shape[1]
    n_pairs = q_ref.shape[1] // LANES

    @pl.when(j == i)
    def _():
        carry_scr[...] = jnp.zeros_like(carry_scr)
        acc_scr[...] = jnp.zeros_like(acc_scr)

    lane = lax.broadcasted_iota(jnp.int32, (1, LANES), 1)
    feat = lax.broadcasted_iota(jnp.int32, (LANES, 1), 0)
    ls = ls_ref[...]

    def sweep(masked):
        if masked:
            row = lax.broadcasted_iota(jnp.int32, (tq, tk), 0)
            col = lax.broadcasted_iota(jnp.int32, (tq, tk), 1)
            tri = col < row
        else:
            tri = None

        def pair_body(p, _):
            off = pl.multiple_of(p * LANES, LANES)
            qp = q_ref[:, pl.ds(off, LANES)]
            kp = k_ref[pl.ds(off, LANES), :].astype(BF16)
            vp = v_ref[pl.ds(off, LANES), :].astype(BF16)
            out = jnp.zeros((tq, LANES), F32)
            for e in range(2):
                h = 2 * p + e
                q_m = (lane < HD_B) if e == 0 else (lane >= HD_B)
                f_m = (feat < HD_B) if e == 0 else (feat >= HD_B)
                qe = jnp.where(q_m, qp, jnp.zeros_like(qp))
                z = jnp.dot(qe, kp, preferred_element_type=F32) + bias_ref[h]
                a, new_carry = _sb_tile(z, ls, carry_scr[h], tri)
                carry_scr[h] = new_carry
                ve = jnp.where(f_m, vp, jnp.zeros_like(vp))
                out = out + lax.dot_general(a.astype(BF16), ve, (((1,), (1,)), ((), ())),
                                            preferred_element_type=F32)
            acc_scr[:, pl.ds(off, LANES)] += out
            return 0

        lax.fori_loop(0, n_pairs, pair_body, 0)

    @pl.when(j == i)
    def _():
        sweep(True)

    @pl.when(j != i)
    def _():
        sweep(False)

    @pl.when(j == 0)
    def _():
        o_ref[...] = acc_scr[...].astype(o_ref.dtype)


def _strict_lower(n):
    r = lax.broadcasted_iota(jnp.int32, (n, n), 0)
    c = lax.broadcasted_iota(jnp.int32, (n, n), 1)
    return (r > c).astype(BF16)


def _attn_prompt_call(q, k, v, bias, batch, seq):
    d_b = q.shape[1]
    assert ATT_TQ == ATT_TK and seq % ATT_TQ == 0
    nq = seq // ATT_TQ
    qi = jnp.asarray([i for i in range(nq) for _ in range(i + 1)], jnp.int32)
    kj = jnp.asarray([j for i in range(nq) for j in range(i, -1, -1)], jnp.int32)
    grid_spec = pltpu.PrefetchScalarGridSpec(
        num_scalar_prefetch=2,
        grid=(batch, qi.shape[0]),
        in_specs=[
            pl.BlockSpec(memory_space=pltpu.SMEM),
            pl.BlockSpec((ATT_TQ, d_b), lambda b, n, qi, kj: (b * nq + qi[n], 0)),
            pl.BlockSpec((None, d_b, ATT_TK), lambda b, n, qi, kj: (b, 0, kj[n])),
            pl.BlockSpec((None, d_b, ATT_TK), lambda b, n, qi, kj: (b, 0, kj[n])),
            pl.BlockSpec((ATT_TK, ATT_TK), lambda b, n, qi, kj: (0, 0)),
        ],
        out_specs=pl.BlockSpec((ATT_TQ, d_b), lambda b, n, qi, kj: (b * nq + qi[n], 0)),
        scratch_shapes=[
            pltpu.VMEM((H_B, ATT_TQ, 1), F32),
            pltpu.VMEM((ATT_TQ, d_b), F32),
        ],
    )
    return pl.pallas_call(
        _attn_prompt_kernel,
        grid_spec=grid_spec,
        out_shape=jax.ShapeDtypeStruct(q.shape, BF16),
        compiler_params=pltpu.CompilerParams(
            dimension_semantics=("arbitrary", "arbitrary"), vmem_limit_bytes=VMEM_LIMIT),
    )(qi, kj, bias, q, k, v, _strict_lower(ATT_TK))


def _attn_sample_kernel(tbl_ref, q_ref, kn_ref, vn_ref, bias_ref, lsn_ref, ls_ref, *rest,
                        t_new, pages):
    k_pages = rest[:pages]
    v_pages = rest[pages:2 * pages]
    o_ref = rest[2 * pages]
    carry_scr, acc_scr = rest[2 * pages + 1:]
    b = pl.program_id(0)
    s = pl.program_id(1)
    rows, d_b = q_ref.shape
    n_new = kn_ref.shape[0]
    nt = (((1,), (1,)), ((), ()))

    r_i = lax.broadcasted_iota(jnp.int32, (rows, d_b), 0)
    l_i = lax.broadcasted_iota(jnp.int32, (rows, d_b), 1)
    head_mask = _div_pow2(r_i, t_new) == _div_pow2(l_i, HD_B)
    qbd = jnp.where(head_mask, q_ref[...], jnp.zeros_like(q_ref[...]))
    bias = bias_ref[...]

    @pl.when(s == 0)
    def _():
        rr = lax.broadcasted_iota(jnp.int32, (rows, n_new), 0)
        cc = lax.broadcasted_iota(jnp.int32, (rows, n_new), 1)
        own = (_div_pow2(cc, t_new) == b) & ((cc & (t_new - 1)) < (rr & (t_new - 1)))
        z = lax.dot_general(qbd, kn_ref[...].astype(BF16), nt, preferred_element_type=F32) + bias
        a, carry = _sb_tile(z, lsn_ref[...], jnp.zeros((rows, 1), F32), own)
        carry_scr[...] = carry
        acc_scr[...] = jnp.dot(a.astype(BF16), vn_ref[...].astype(BF16), preferred_element_type=F32)

    ls = ls_ref[...]
    for r in range(pages):
        z = jnp.dot(qbd, k_pages[r][...].astype(BF16), preferred_element_type=F32) + bias
        a, carry = _sb_tile(z, ls, carry_scr[...], None)
        carry_scr[...] = carry
        acc_scr[...] += lax.dot_general(a.astype(BF16), v_pages[r][...].astype(BF16), nt,
                                        preferred_element_type=F32)

    @pl.when(s == pl.num_programs(1) - 1)
    def _():
        accm = jnp.where(head_mask, acc_scr[...], 0.0)
        out = accm[0:t_new, :]
        for h in range(1, rows // t_new):
            out = out + accm[h * t_new:(h + 1) * t_new, :]
        o_ref[...] = out.astype(o_ref.dtype)


def _attn_sample_call(q_rep, k_new, v_new, bias_col, cache_k, cache_v, table):
    db, rows, d_b = q_rep.shape
    t_new = rows // H_B
    n_new = k_new.shape[0]
    page = cache_k.shape[2]
    n_pages = table.shape[0] // db
    pages = PAGES_PER_STEP
    assert n_pages % pages == 0
    n_steps = n_pages // pages

    def page_spec(r):
        return pl.BlockSpec(
            (None, d_b, page),
            lambda b, s, tbl: (tbl[b * n_pages + n_pages - 1 - (s * pages + r)], 0, 0))

    const = lambda shape: pl.BlockSpec(shape, lambda b, s, tbl: (0,) * len(shape))
    grid_spec = pltpu.PrefetchScalarGridSpec(
        num_scalar_prefetch=1,
        grid=(db, n_steps),
        in_specs=[
            pl.BlockSpec((None, rows, d_b), lambda b, s, tbl: (b, 0, 0)),
            const((n_new, d_b)), const((n_new, d_b)),
            const((rows, 1)), const((n_new, n_new)), const((page, page)),
        ] + [page_spec(r) for r in range(pages)] * 2,
        out_specs=pl.BlockSpec((None, t_new, d_b), lambda b, s, tbl: (b, 0, 0)),
        scratch_shapes=[
            pltpu.VMEM((rows, 1), F32),
            pltpu.VMEM((rows, d_b), F32),
        ],
    )
    return pl.pallas_call(
        functools.partial(_attn_sample_kernel, t_new=t_new, pages=pages),
        grid_spec=grid_spec,
        out_shape=jax.ShapeDtypeStruct((db, t_new, d_b), BF16),
        compiler_params=pltpu.CompilerParams(
            dimension_semantics=("arbitrary", "arbitrary"), vmem_limit_bytes=VMEM_LIMIT),
    )(table, q_rep, k_new, v_new, bias_col, _strict_lower(n_new), _strict_lower(page),
      *([cache_k] * pages), *([cache_v] * pages))


def _post_kernel(x_ref, o_ref, u_ref, vn_ref, ga_ref, gb_ref, g1_ref, mix_ref, bcol_ref,
                 wa_ref, wb_ref, wo_ref, lng_ref, lnb_ref, y_ref, ya_scr, *, alpha):
    tm = x_ref.shape[0]
    r = lax.broadcasted_iota(jnp.int32, (CHUNK, CHUNK), 0)
    c = lax.broadcasted_iota(jnp.int32, (CHUNK, CHUNK), 1)
    causal = r >= c
    cg = vn_ref.shape[1] // G_A
    for g in range(G_A):
        w = jnp.where(causal, mix_ref[g], 0.0).astype(BF16)
        bias = bcol_ref[:, g:g + 1]
        cols = slice(g * cg, (g + 1) * cg)
        for ch in range(tm // CHUNK):
            rows = slice(ch * CHUNK, (ch + 1) * CHUNK)
            mixed = jnp.dot(w, vn_ref[rows, cols].astype(BF16), preferred_element_type=F32) + bias
            ya_scr[rows, cols] = (u_ref[rows, cols].astype(F32) * mixed).astype(BF16)
    y_a = jnp.dot(ya_scr[...], wa_ref[...], preferred_element_type=F32)
    y_b = jnp.dot(o_ref[...], wb_ref[...], preferred_element_type=F32)
    merged = ga_ref[...].astype(F32) * y_a + gb_ref[...].astype(F32) * y_b
    t = jnp.dot(merged.astype(BF16), wo_ref[...], preferred_element_type=F32)
    y_ref[...] = _ln(alpha * x_ref[...] + g1_ref[...] * t, lng_ref[...], lnb_ref[...])


def _post_call(x2, o, u, vn, ga, gb, mods, per_row, tiles_per_batch, tm, mix, bcol,
               w_a, w_b, w_o, ln_g, ln_b, alpha):
    n, d = x2.shape
    d_a, d_b = u.shape[1], o.shape[1]
    row = lambda c: pl.BlockSpec((tm, c), lambda i: (i, 0))
    return pl.pallas_call(
        functools.partial(_post_kernel, alpha=alpha),
        grid=(n // tm,),
        in_specs=[
            row(d), row(d_b), row(d_a), row(d_a), row(d), row(d),
            _mod_spec(per_row, tm, d, tiles_per_batch, 2),
            _const_spec(mix.shape), _const_spec(bcol.shape),
            _const_spec(w_a.shape), _const_spec(w_b.shape), _const_spec(w_o.shape),
            _const_spec((1, d)), _const_spec((1, d)),
        ],
        out_specs=row(d),
        out_shape=jax.ShapeDtypeStruct((n, d), F32),
        scratch_shapes=[pltpu.VMEM((tm, d_a), BF16)],
        compiler_params=pltpu.CompilerParams(
            dimension_semantics=("arbitrary",), vmem_limit_bytes=VMEM_LIMIT),
    )(x2, o, u, vn, ga, gb, mods, mix, bcol, w_a, w_b, w_o, ln_g.reshape(1, d), ln_b.reshape(1, d))


def _mlp_kernel(x_ref, sh_ref, sc_ref, g2_ref, wu_ref, bu_ref, wd_ref, bd_ref, lng_ref, lnb_ref,
                y_ref, *, alpha, ff_chunk):
    x = x_ref[...]
    h = (x * (1.0 + sc_ref[...]) + sh_ref[...]).astype(BF16)
    d_ff = wu_ref.shape[1]
    f = jnp.zeros(x.shape, F32)
    for c in range(d_ff // ff_chunk):
        cols = slice(c * ff_chunk, (c + 1) * ff_chunk)
        hid = jnp.dot(h, wu_ref[:, cols], preferred_element_type=F32) + bu_ref[:, cols]
        hid = jnp.maximum(hid, 0.0)
        f = f + jnp.dot((hid * hid).astype(BF16), wd_ref[cols, :], preferred_element_type=F32)
    f = f + bd_ref[...]
    y_ref[...] = _ln(alpha * x + g2_ref[...] * f, lng_ref[...], lnb_ref[...])


def _mlp_call(x2, mods, per_row, tiles_per_batch, tm, w_up, b_up, w_down, b_down, ln_g, ln_b, alpha):
    n, d = x2.shape
    d_ff = w_up.shape[1]
    row = pl.BlockSpec((tm, d), lambda i: (i, 0))
    return pl.pallas_call(
        functools.partial(_mlp_kernel, alpha=alpha, ff_chunk=1024),
        grid=(n // tm,),
        in_specs=[
            row,
            _mod_spec(per_row, tm, d, tiles_per_batch, 3),
            _mod_spec(per_row, tm, d, tiles_per_batch, 4),
            _mod_spec(per_row, tm, d, tiles_per_batch, 5),
            _const_spec((d, d_ff)), _const_spec((1, d_ff)),
            _const_spec((d_ff, d)), _const_spec((1, d)),
            _const_spec((1, d)), _const_spec((1, d)),
        ],
        out_specs=row,
        out_shape=jax.ShapeDtypeStruct((n, d), F32),
        compiler_params=pltpu.CompilerParams(
            dimension_semantics=("arbitrary",), vmem_limit_bytes=VMEM_LIMIT),
    )(x2, mods, mods, mods, w_up, b_up.reshape(1, d_ff), w_down, b_down.reshape(1, d),
      ln_g.reshape(1, d), ln_b.reshape(1, d))


def kernel(x_prompt, x_sample, cache_k, cache_v, page_table, c_prompt, c_sample, w_ada, b_ada, w_in, b_sb, ln_v_g, ln_v_b, w_s, b_s, w_a, w_b, w_o, ln1_g, ln1_b, w_up, b_up, w_down, b_down, ln2_g, ln2_b):
    batch, seq, d = x_prompt.shape
    db, t_new, _ = x_sample.shape
    depth, n_pool, page, n_heads, hd = cache_k.shape
    assert (n_heads, hd) == (H_B, HD_B) and page == CHUNK
    d_a = ln_v_g.shape[1]
    d_b = n_heads * hd
    n_pages = page_table.shape[1]
    alpha = (2 * depth) ** 0.25
    n_s = db * t_new
    assert CHUNK % t_new == 0 and n_s % CHUNK == 0

    n_c = batch + db
    pad = (-n_c) % 8
    c_all = jnp.concatenate([c_prompt, c_sample, jnp.zeros((pad, d), F32)], axis=0)
    mods = _mods_call(c_all, w_ada, b_ada)

    cache_k2 = jnp.transpose(cache_k, (0, 1, 3, 4, 2)).reshape(depth * n_pool, d_b, page)
    cache_v2 = jnp.transpose(cache_v, (0, 1, 3, 4, 2)).reshape(depth * n_pool, d_b, page)
    table = page_table.reshape(-1).astype(jnp.int32)

    tm_p = 256
    tm_s = CHUNK
    tpb = seq // tm_p
    eye = jnp.eye(CHUNK // t_new, dtype=F32)

    yp = x_prompt.reshape(batch * seq, d)
    ys = x_sample.reshape(n_s, d)
    outs = [[] for _ in range(6)]
    for l in range(depth):
        w_in_l = w_in[l].astype(BF16)
        kv0 = 2 * d_a + d_b
        w_kvt_l = w_in_l[:, kv0:kv0 + 2 * d_b].T
        w_a_l, w_b_l, w_o_l = w_a[l].astype(BF16), w_b[l].astype(BF16), w_o[l].astype(BF16)
        w_up_l, w_down_l = w_up[l].astype(BF16), w_down[l].astype(BF16)
        mods_p = mods[l, :batch].reshape(batch, 6, 1, d)
        mods_s = jnp.repeat(mods[l, batch:n_c], t_new, axis=0)
        mix_s = jnp.stack([jnp.kron(eye, w_s[l, g, :t_new, :t_new]) for g in range(G_A)])
        bcol_p = b_s[l].T
        bcol_s = jnp.tile(b_s[l, :, :t_new].T, (CHUNK // t_new, 1))
        bias_col = jnp.repeat(b_sb[l], t_new).reshape(n_heads * t_new, 1)

        u, vn, q, k, v, ga, gb = _inproj_call(yp, mods_p, False, tpb, tm_p, w_in_l, w_kvt_l,
                                              ln_v_g[l], ln_v_b[l], d_a, d_b, True)
        o = _attn_prompt_call(q, k, v, b_sb[l], batch, seq)
        x1 = _post_call(yp, o, u, vn, ga, gb, mods_p, False, tpb, tm_p, w_s[l], bcol_p,
                        w_a_l, w_b_l, w_o_l, ln1_g[l], ln1_b[l], alpha)
        yp = _mlp_call(x1, mods_p, False, tpb, tm_p, w_up_l, b_up[l], w_down_l, b_down[l],
                       ln2_g[l], ln2_b[l], alpha)
        outs[0].append(k.reshape(batch, n_heads, hd, seq))
        outs[1].append(v.reshape(batch, n_heads, hd, seq))
        outs[2].append(vn.reshape(batch, seq, d_a)[:, seq - CHUNK:])

        u, vn, q, k, v, ga, gb = _inproj_call(ys, mods_s, True, 1, tm_s, w_in_l, w_kvt_l,
                                              ln_v_g[l], ln_v_b[l], d_a, d_b, False)
        q_rep = jnp.tile(q.reshape(db, t_new, d_b), (1, n_heads, 1))
        o = _attn_sample_call(q_rep, k, v, bias_col, cache_k2, cache_v2, table + l * n_pool)
        x1 = _post_call(ys, o.reshape(n_s, d_b), u, vn, ga, gb, mods_s, True, 1, tm_s, mix_s, bcol_s,
                        w_a_l, w_b_l, w_o_l, ln1_g[l], ln1_b[l], alpha)
        ys = _mlp_call(x1, mods_s, True, 1, tm_s, w_up_l, b_up[l], w_down_l, b_down[l],
                       ln2_g[l], ln2_b[l], alpha)
        outs[3].append(k.reshape(db, t_new, n_heads, hd))
        outs[4].append(v.reshape(db, t_new, n_heads, hd))
        outs[5].append(vn.reshape(db, t_new, d_a))

    to_rows = lambda xs: jnp.transpose(jnp.stack(xs), (0, 1, 4, 2, 3))
    return (yp.reshape(batch, seq, d), ys.reshape(db, t_new, d),
            to_rows(outs[0]), to_rows(outs[1]), jnp.stack(outs[2]),
            jnp.stack(outs[3]), jnp.stack(outs[4]), jnp.stack(outs[5]))
```

```python
import functools

import jax
import jax.numpy as jnp
from jax import lax
from jax.experimental import pallas as pl
from jax.experimental.pallas import tpu as pltpu

F32 = jnp.float32
BF16 = jnp.bfloat16

CHUNK = 128
G_A = 4
H_B = 8
HD_B = 64
LN_EPS = 1e-5
LOG2E = 1.4426950408889634
LANES = 128
ATT_TQ = 512
ATT_TK = 512
ATT_SUB = 256
SAMPLE_GROUP = 8
VMEM_LIMIT = 56 * 1024 * 1024


def _ln(x, g, b):
    mu = jnp.mean(x, axis=-1, keepdims=True)
    xc = x - mu
    var = jnp.mean(xc * xc, axis=-1, keepdims=True)
    return xc * lax.rsqrt(var + LN_EPS) * g + b


def _gelu(x):
    return 0.5 * x * (1.0 + lax.erf(x * (2.0 ** -0.5)))


def _sigmoid(x):
    return 1.0 / (1.0 + jnp.exp(-x))


def _div_pow2(x, n):
    assert n & (n - 1) == 0
    return x >> (n.bit_length() - 1)


def _mods_kernel(c_ref, w_ref, b_ref, o_ref):
    c = c_ref[...]
    s = c * _sigmoid(c)
    o_ref[...] = jnp.dot(s, w_ref[...], preferred_element_type=F32,
                         precision=lax.Precision.HIGHEST) + b_ref[...]


def _mods_call(c_all, w_ada, b_ada):
    depth, d, n6 = w_ada.shape
    rows = c_all.shape[0]
    tn = 1536
    return pl.pallas_call(
        _mods_kernel,
        grid=(depth, n6 // tn),
        in_specs=[
            pl.BlockSpec((rows, d), lambda l, j: (0, 0)),
            pl.BlockSpec((None, d, tn), lambda l, j: (l, 0, j)),
            pl.BlockSpec((None, 1, tn), lambda l, j: (l, 0, j)),
        ],
        out_specs=pl.BlockSpec((None, rows, tn), lambda l, j: (l, 0, j)),
        out_shape=jax.ShapeDtypeStruct((depth, rows, n6), F32),
        compiler_params=pltpu.CompilerParams(
            dimension_semantics=("arbitrary", "arbitrary"), vmem_limit_bytes=VMEM_LIMIT),
    )(c_all, w_ada, b_ada.reshape(depth, 1, n6))


def _mod_spec(l, per_row, tm, d, tiles_per_batch, k):
    if per_row:
        return pl.BlockSpec((None, tm, d), lambda i: (l, i, k))
    return pl.BlockSpec((None, None, None, 1, d), lambda i: (l, i // tiles_per_batch, k, 0, 0))


def _layer_spec(l, shape):
    nd = len(shape)
    return pl.BlockSpec((None,) + tuple(shape), lambda i: (l,) + (0,) * nd)


def _inproj_kernel(x_ref, sh_ref, sc_ref, w_ref, lng_ref, lnb_ref, *rest,
                   d_a, d_b, d_model, kv_t):
    u_ref, vn_ref, q_ref, k_ref, v_ref, ga_ref, gb_ref = rest[-7:]
    h = (x_ref[...] * (1.0 + sc_ref[...]) + sh_ref[...]).astype(BF16)

    def seg(a, n):
        return jnp.dot(h, w_ref[:, a:a + n], preferred_element_type=F32)

    o = 0
    u_ref[...] = _gelu(seg(o, d_a)).astype(u_ref.dtype)
    o += d_a
    vn_ref[...] = _ln(_gelu(seg(o, d_a)), lng_ref[...], lnb_ref[...])
    o += d_a
    q_ref[...] = (seg(o, d_b) * (HD_B ** -0.5 * LOG2E)).astype(q_ref.dtype)
    o += d_b
    if kv_t:
        k_ref[...] = jnp.broadcast_to(seg(o, d_b).T, k_ref.shape)
        v_ref[...] = jnp.broadcast_to(seg(o + d_b, d_b).T, v_ref.shape)
    else:
        k_ref[...] = seg(o, d_b)
        v_ref[...] = seg(o + d_b, d_b)
    o += 2 * d_b
    ga_ref[...] = _sigmoid(seg(o, d_model)).astype(ga_ref.dtype)
    o += d_model
    gb_ref[...] = _sigmoid(seg(o, d_model)).astype(gb_ref.dtype)


def _inproj_call(l, x2, mods, per_row, tiles_per_batch, tm, w_in, ln_g, ln_b, d_a, d_b, kv_prev):
    n, d = x2.shape
    depth, _, d_in = w_in.shape
    row = lambda c: pl.BlockSpec((tm, c), lambda i: (i, 0))
    if kv_prev is not None:
        tpb = tiles_per_batch
        kv_shape = jax.ShapeDtypeStruct((depth, n // (tpb * tm), d_b, tpb * tm), F32)
        if kv_prev:
            kv_spec = pl.BlockSpec((None, None, d_b, tm), lambda i: (l, i // tpb, 0, i % tpb))
        else:
            kv_spec = pl.BlockSpec((depth, None, d_b, tm), lambda i: (0, i // tpb, 0, i % tpb))
    else:
        kv_prev = ()
        kv_shape = jax.ShapeDtypeStruct((n, d_b), F32)
        kv_spec = row(d_b)
    n_in = 6
    out_shape = (
        jax.ShapeDtypeStruct((n, d_a), BF16),
        jax.ShapeDtypeStruct((n, d_a), F32),
        jax.ShapeDtypeStruct((n, d_b), BF16),
        kv_shape,
        kv_shape,
        jax.ShapeDtypeStruct((n, d), BF16),
        jax.ShapeDtypeStruct((n, d), BF16),
    )
    return pl.pallas_call(
        functools.partial(_inproj_kernel, d_a=d_a, d_b=d_b, d_model=d, kv_t=kv_shape.ndim == 4),
        grid=(n // tm,),
        in_specs=[
            row(d),
            _mod_spec(l, per_row, tm, d, tiles_per_batch, 0),
            _mod_spec(l, per_row, tm, d, tiles_per_batch, 1),
            _layer_spec(l, (d, d_in)),
            _layer_spec(l, (1, d_a)),
            _layer_spec(l, (1, d_a)),
        ] + [pl.BlockSpec(memory_space=pl.ANY)] * len(kv_prev),
        out_specs=(row(d_a), row(d_a), row(d_b), kv_spec, kv_spec, row(d), row(d)),
        out_shape=out_shape,
        input_output_aliases={n_in + a: 3 + a for a in range(len(kv_prev))},
        compiler_params=pltpu.CompilerParams(
            dimension_semantics=("arbitrary",), vmem_limit_bytes=VMEM_LIMIT),
    )(x2, mods, mods, w_in, ln_g, ln_b, *kv_prev)


def _sb_softplus(z, valid):
    sign = jnp.uint32(0x80000000)
    neg_abs = lax.bitcast_convert_type(lax.bitcast_convert_type(z, jnp.uint32) | sign, F32)
    sp = jnp.maximum(z, 0.0) + jnp.log(1.0 + jnp.exp2(neg_abs)) * LOG2E
    if valid is not None:
        sp = jnp.where(valid, sp, 0.0)
    return sp


def _sb_weights(own, excl, carry, valid):
    a = jnp.exp2(own - excl - carry)
    if valid is not None:
        a = jnp.where(valid, a, 0.0)
    return a


def _sb_tile(z, ls, carry, valid):
    sp = _sb_softplus(z, valid)
    excl = jnp.dot(sp.astype(BF16), ls, preferred_element_type=F32)
    a = _sb_weights(z - sp, excl, carry, valid)
    return a, carry + excl[:, :1] + sp[:, :1]


def _prompt_step(i, j, bias_ref, q_ref, k_ref, v_ref, ls_ref, o_ref, carry_scr, acc_scr):
    tq, tk = q_ref.shape[0], k_ref.shape[1]
    sub_k = ls_ref.shape[0]
    n_pairs = q_ref.shape[1] // LANES

    @pl.when(j == i)
    def _():
        carry_scr[...] = jnp.zeros_like(carry_scr)
        acc_scr[...] = jnp.zeros_like(acc_scr)

    lane = lax.broadcasted_iota(jnp.int32, (1, LANES), 1)
    feat = lax.broadcasted_iota(jnp.int32, (LANES, 1), 0)
    ls = ls_ref[...]

    def sweep(masked):
        units = [(sub, h) for sub in reversed(range(tk // sub_k)) for h in range(2 * n_pairs)]
        st = {u: dict() for u in units}
        valid, k16, v16 = {}, {}, {}

        def tri(sub):
            if not masked:
                return None
            if sub not in valid:
                row = lax.broadcasted_iota(jnp.int32, (tq, sub_k), 0)
                col = lax.broadcasted_iota(jnp.int32, (tq, sub_k), 1)
                valid[sub] = col + sub * sub_k < row
            return valid[sub]

        def cols(h):
            return slice((h // 2) * LANES, (h // 2 + 1) * LANES)

        def pair_bf16(cache, ref, u):
            sub, h = u
            if (sub, h // 2) not in cache:
                cache[sub, h // 2] = ref[cols(h), sub * sub_k:(sub + 1) * sub_k].astype(BF16)
            return cache[sub, h // 2]

        def logits(u):
            h = u[1]
            q_m = (lane < HD_B) if h % 2 == 0 else (lane >= HD_B)
            x0 = HD_B if h % 2 == 0 else 0
            b2 = jnp.full((1, LANES), bias_ref[h] * LOG2E, F32)
            b_hi = b2.astype(BF16).astype(F32)
            b_row = jnp.where(lane == x0, b_hi, jnp.where(lane == x0 + 1, b2 - b_hi, 0.0))
            qp = q_ref[:, cols(h)]
            qe = jnp.where(q_m, qp, jnp.broadcast_to(b_row.astype(BF16), qp.shape))
            kp = pair_bf16(k16, k_ref, u)
            ke = jnp.where((feat == x0) | (feat == x0 + 1), jnp.ones_like(kp), kp)
            st[u]["z"] = jnp.dot(qe, ke, preferred_element_type=F32)

        def softplus(u):
            z = st[u].pop("z")
            sp = _sb_softplus(z, tri(u[0]))
            st[u]["own"] = z - sp
            st[u]["sp16"] = sp.astype(BF16)
            st[u]["sp0"] = sp[:, :1]

        def suffix(u):
            st[u]["excl"] = jnp.dot(st[u].pop("sp16"), ls, preferred_element_type=F32)

        def weights(u):
            excl = st[u].pop("excl")
            carry = carry_scr[u[1]]
            st[u]["a16"] = _sb_weights(st[u].pop("own"), excl, carry, tri(u[0])).astype(BF16)
            carry_scr[u[1]] = carry + excl[:, :1] + st[u].pop("sp0")

        def values(u):
            h = u[1]
            f_m = (feat < HD_B) if h % 2 == 0 else (feat >= HD_B)
            vp = pair_bf16(v16, v_ref, u)
            ve = jnp.where(f_m, vp, jnp.zeros_like(vp))
            acc_scr[:, cols(h)] += lax.dot_general(st[u].pop("a16"), ve, (((1,), (1,)), ((), ())),
                                                   preferred_element_type=F32)

        stages = (logits, softplus, suffix, weights, values)
        for t in range(len(units) + len(stages) - 1):
            for s in reversed(range(len(stages))):
                if 0 <= t - s < len(units):
                    stages[s](units[t - s])

    @pl.when(j == i)
    def _():
        sweep(True)

    @pl.when(j != i)
    def _():
        sweep(False)

    @pl.when(j == 0)
    def _():
        o_ref[...] = acc_scr[...].astype(o_ref.dtype)


def _strict_lower(n):
    r = lax.broadcasted_iota(jnp.int32, (n, n), 0)
    c = lax.broadcasted_iota(jnp.int32, (n, n), 1)
    return (r > c).astype(BF16)


_NT = (((1,), (1,)), ((), ()))


def _sample_queries(q_ref, bias_ref, t_new):
    rows, d_b = q_ref.shape
    r_i = lax.broadcasted_iota(jnp.int32, (rows, d_b), 0)
    l_i = lax.broadcasted_iota(jnp.int32, (rows, d_b), 1)
    head_mask = _div_pow2(r_i, t_new) == _div_pow2(l_i, HD_B)
    qbd = jnp.where(head_mask, q_ref[...], jnp.zeros_like(q_ref[...]))
    return qbd, bias_ref[...] * LOG2E, head_mask


def _sample_begin(b, t_new, q_ref, kn_ref, vn_ref, bias_ref, lsn_ref, carry_scr, acc_scr):
    rows, n_new = q_ref.shape[0], kn_ref.shape[0]
    qbd, bias, _ = _sample_queries(q_ref, bias_ref, t_new)
    rr = lax.broadcasted_iota(jnp.int32, (rows, n_new), 0)
    cc = lax.broadcasted_iota(jnp.int32, (rows, n_new), 1)
    own = (_div_pow2(cc, t_new) == b) & ((cc & (t_new - 1)) < (rr & (t_new - 1)))
    z = lax.dot_general(qbd, kn_ref[...].astype(BF16), _NT, preferred_element_type=F32) + bias
    a, carry = _sb_tile(z, lsn_ref[...], jnp.zeros((rows, 1), F32), own)
    carry_scr[...] = carry
    acc_scr[...] = jnp.dot(a.astype(BF16), vn_ref[...].astype(BF16), preferred_element_type=F32)


def _sample_page_stages(t_new, q_ref, bias_ref, ls_ref, k_pages, v_pages, carry_scr, acc_scr):
    pages = len(k_pages)
    rows = q_ref.shape[0]
    st = {}

    def logits():
        st["q"], bias, _ = _sample_queries(q_ref, bias_ref, t_new)
        st["z"] = [jnp.dot(st["q"], k_pages[r][...].astype(BF16), preferred_element_type=F32) + bias
                   for r in range(pages)]

    def softplus():
        st["sp"] = [_sb_softplus(z, None) for z in st["z"]]
        st["own"] = [z - sp for z, sp in zip(st.pop("z"), st["sp"])]

    def suffix():
        st["excl"] = jnp.dot(jnp.concatenate(st["sp"], axis=0).astype(BF16), ls_ref[...],
                             preferred_element_type=F32)

    def weights():
        carry = carry_scr[...]
        st["a"] = []
        for r in range(pages):
            excl = st["excl"][r * rows:(r + 1) * rows]
            st["a"].append(_sb_weights(st["own"][r], excl, carry, None).astype(BF16))
            carry = carry + excl[:, :1] + st["sp"][r][:, :1]
        carry_scr[...] = carry

    def values():
        acc = jnp.zeros(acc_scr.shape, F32)
        for r in range(pages):
            acc = acc + lax.dot_general(st["a"][r], v_pages[r][...].astype(BF16), _NT,
                                        preferred_element_type=F32)
        acc_scr[...] += acc

    return [logits, softplus, suffix, weights, values]


def _sample_end(t_new, q_ref, bias_ref, o_ref, acc_scr):
    rows = q_ref.shape[0]
    _, _, head_mask = _sample_queries(q_ref, bias_ref, t_new)
    accm = jnp.where(head_mask, acc_scr[...], 0.0)
    out = accm[0:t_new, :]
    for h in range(1, rows // t_new):
        out = out + accm[h * t_new:(h + 1) * t_new, :]
    o_ref[...] = out.astype(o_ref.dtype)


def _page_copies(page_ref, step, slot, ck_ref, cv_ref, kbuf, vbuf, sem):
    pages = kbuf.shape[1]
    copies = []
    for r in range(pages):
        pid = page_ref[step * pages + r]
        copies.append(pltpu.make_async_copy(ck_ref.at[pid], kbuf.at[slot, r], sem.at[0, slot]))
        copies.append(pltpu.make_async_copy(cv_ref.at[pid], vbuf.at[slot, r], sem.at[1, slot]))
    return copies


def _attn_kernel(qi_ref, kj_ref, seq_ref, page_ref, bias_ref, q_ref, k_ref, v_ref, ls_ref,
                 qs_ref, kn_ref, vn_ref, bcol_ref, lsn_ref, lsp_ref, ck_ref, cv_ref,
                 o_ref, os_ref, carry_scr, acc_scr, carry_s, acc_s, kbuf, vbuf, sem,
                 *, t_new, seq_steps, sample_steps):
    pages = kbuf.shape[1]
    n = pl.program_id(1)
    lin = pl.program_id(0) * pl.num_programs(1) + n
    active = lin < sample_steps
    s = lax.rem(lin, seq_steps)
    slot = lax.rem(lin, 2)

    def copies(step, into):
        return _page_copies(page_ref, step, into, ck_ref, cv_ref, kbuf, vbuf, sem)

    @pl.when(lin == 0)
    def _():
        for c in copies(0, 0):
            c.start()

    @pl.when(lin + 1 < sample_steps)
    def _():
        for c in copies(lin + 1, 1 - slot):
            c.start()

    @pl.when(active & (s == 0))
    def _():
        _sample_begin(seq_ref[lin], t_new, qs_ref, kn_ref, vn_ref, bcol_ref, lsn_ref,
                      carry_s, acc_s)

    _prompt_step(qi_ref[n], kj_ref[n], bias_ref, q_ref, k_ref, v_ref, ls_ref, o_ref,
                 carry_scr, acc_scr)

    @pl.when(active)
    def _():
        for c in copies(lin, slot):
            c.wait()
        k_pages = [kbuf.at[slot, r] for r in range(pages)]
        v_pages = [vbuf.at[slot, r] for r in range(pages)]
        group = min(pages, SAMPLE_GROUP)
        chains = [_sample_page_stages(t_new, qs_ref, bcol_ref, lsp_ref, k_pages[g:g + group],
                                      v_pages[g:g + group], carry_s, acc_s)
                  for g in range(0, pages, group)]
        n_stage = len(chains[0])
        for t in range(len(chains) + n_stage - 1):
            for st in reversed(range(n_stage)):
                if 0 <= t - st < len(chains):
                    chains[t - st][st]()

    @pl.when(active & (s == seq_steps - 1))
    def _():
        _sample_end(t_new, qs_ref, bcol_ref, os_ref, acc_s)


def _attn_call(l, q, k, v, bias, batch, seq, q_rep, k_new, v_new, bias_col, cache_k, cache_v,
               table):
    d_b = q.shape[1]
    assert ATT_TQ == ATT_TK and seq % ATT_TQ == 0 and ATT_TK % ATT_SUB == 0
    nq = seq // ATT_TQ
    qi = jnp.asarray([i for i in range(nq) for _ in range(i + 1)], jnp.int32)
    kj = jnp.asarray([j for i in range(nq) for j in range(i, -1, -1)], jnp.int32)
    n_steps = qi.shape[0]

    db, rows, _ = q_rep.shape
    t_new = rows // H_B
    n_new = k_new.shape[0]
    page = cache_k.shape[2]
    n_pages = table.shape[0] // db
    pages = min(p for p in range(1, n_pages + 1)
                if n_pages % p == 0 and db * (n_pages // p) <= batch * n_steps)
    seq_steps = n_pages // pages
    sample_steps = db * seq_steps

    lin = jnp.minimum(jnp.arange(batch * n_steps, dtype=jnp.int32), sample_steps - 1)
    seq_of = lin // seq_steps
    logical = n_pages - 1 - ((lin % seq_steps)[:, None] * pages
                             + jnp.arange(pages, dtype=jnp.int32)[None, :])
    page_of = table.reshape(db, n_pages)[seq_of[:, None], logical].reshape(-1)

    def per_seq(shape):
        return pl.BlockSpec((None,) + shape,
                            lambda b, n, qi, kj, sq, pg: (sq[b * n_steps + n], 0, 0))

    const = lambda shape: pl.BlockSpec(shape, lambda b, n, qi, kj, sq, pg: (0,) * len(shape))
    grid_spec = pltpu.PrefetchScalarGridSpec(
        num_scalar_prefetch=4,
        grid=(batch, n_steps),
        in_specs=[
            pl.BlockSpec(memory_space=pltpu.SMEM),
            pl.BlockSpec((ATT_TQ, d_b), lambda b, n, qi, kj, sq, pg: (b * nq + qi[n], 0)),
            pl.BlockSpec((None, None, d_b, ATT_TK), lambda b, n, qi, kj, sq, pg: (l, b, 0, kj[n])),
            pl.BlockSpec((None, None, d_b, ATT_TK), lambda b, n, qi, kj, sq, pg: (l, b, 0, kj[n])),
            const((ATT_SUB, ATT_SUB)),
            per_seq((rows, d_b)),
            const((n_new, d_b)), const((n_new, d_b)),
            const((rows, 1)), const((n_new, n_new)), const((page, page)),
            pl.BlockSpec(memory_space=pl.ANY), pl.BlockSpec(memory_space=pl.ANY),
        ],
        out_specs=(
            pl.BlockSpec((ATT_TQ, d_b), lambda b, n, qi, kj, sq, pg: (b * nq + qi[n], 0)),
            per_seq((t_new, d_b)),
        ),
        scratch_shapes=[
            pltpu.VMEM((H_B, ATT_TQ, 1), F32),
            pltpu.VMEM((ATT_TQ, d_b), F32),
            pltpu.VMEM((rows, 1), F32),
            pltpu.VMEM((rows, d_b), F32),
            pltpu.VMEM((2, pages, d_b, page), F32),
            pltpu.VMEM((2, pages, d_b, page), F32),
            pltpu.SemaphoreType.DMA((2, 2)),
        ],
    )
    return pl.pallas_call(
        functools.partial(_attn_kernel, t_new=t_new, seq_steps=seq_steps,
                          sample_steps=sample_steps),
        grid_spec=grid_spec,
        out_shape=(jax.ShapeDtypeStruct(q.shape, BF16),
                   jax.ShapeDtypeStruct((db, t_new, d_b), BF16)),
        compiler_params=pltpu.CompilerParams(
            dimension_semantics=("arbitrary", "arbitrary"), vmem_limit_bytes=VMEM_LIMIT),
    )(qi, kj, seq_of, page_of, bias, q, k, v, _strict_lower(ATT_SUB),
      q_rep, k_new, v_new, bias_col, _strict_lower(n_new), _strict_lower(page),
      cache_k, cache_v)


def _post_kernel(x_ref, o_ref, u_ref, vn_ref, ga_ref, gb_ref, g1_ref, mix_ref, bcol_ref,
                 wa_ref, wb_ref, wo_ref, lng_ref, lnb_ref, y_ref, ya_scr, *, alpha):
    tm = x_ref.shape[0]
    r = lax.broadcasted_iota(jnp.int32, (CHUNK, CHUNK), 0)
    c = lax.broadcasted_iota(jnp.int32, (CHUNK, CHUNK), 1)
    causal = r >= c
    cg = vn_ref.shape[1] // G_A
    for g in range(G_A):
        w = jnp.where(causal, mix_ref[g], 0.0).astype(BF16)
        bias = bcol_ref[:, g:g + 1]
        cols = slice(g * cg, (g + 1) * cg)
        for ch in range(tm // CHUNK):
            rows = slice(ch * CHUNK, (ch + 1) * CHUNK)
            mixed = jnp.dot(w, vn_ref[rows, cols].astype(BF16), preferred_element_type=F32) + bias
            ya_scr[rows, cols] = (u_ref[rows, cols].astype(F32) * mixed).astype(BF16)
    y_a = jnp.dot(ya_scr[...], wa_ref[...], preferred_element_type=F32)
    y_b = jnp.dot(o_ref[...], wb_ref[...], preferred_element_type=F32)
    merged = ga_ref[...].astype(F32) * y_a + gb_ref[...].astype(F32) * y_b
    t = jnp.dot(merged.astype(BF16), wo_ref[...], preferred_element_type=F32)
    y_ref[...] = _ln(alpha * x_ref[...] + g1_ref[...] * t, lng_ref[...], lnb_ref[...])


def _post_call(l, x2, o, u, vn, ga, gb, mods, per_row, tiles_per_batch, tm, mix, bcol,
               w_a, w_b, w_o, ln_g, ln_b, alpha):
    n, d = x2.shape
    d_a, d_b = u.shape[1], o.shape[1]
    row = lambda c: pl.BlockSpec((tm, c), lambda i: (i, 0))
    return pl.pallas_call(
        functools.partial(_post_kernel, alpha=alpha),
        grid=(n // tm,),
        in_specs=[
            row(d), row(d_b), row(d_a), row(d_a), row(d), row(d),
            _mod_spec(l, per_row, tm, d, tiles_per_batch, 2),
            _layer_spec(l, mix.shape[1:]), _layer_spec(l, bcol.shape[1:]),
            _layer_spec(l, w_a.shape[1:]), _layer_spec(l, w_b.shape[1:]),
            _layer_spec(l, w_o.shape[1:]),
            _layer_spec(l, (1, d)), _layer_spec(l, (1, d)),
        ],
        out_specs=row(d),
        out_shape=jax.ShapeDtypeStruct((n, d), F32),
        scratch_shapes=[pltpu.VMEM((tm, d_a), BF16)],
        compiler_params=pltpu.CompilerParams(
            dimension_semantics=("arbitrary",), vmem_limit_bytes=VMEM_LIMIT),
    )(x2, o, u, vn, ga, gb, mods, mix, bcol, w_a, w_b, w_o, ln_g, ln_b)


def _mlp_kernel(x_ref, sh_ref, sc_ref, g2_ref, wu_ref, bu_ref, wd_ref, bd_ref, lng_ref, lnb_ref,
                y_ref, *, alpha, ff_chunk):
    x = x_ref[...]
    h = (x * (1.0 + sc_ref[...]) + sh_ref[...]).astype(BF16)
    d_ff = wu_ref.shape[1]
    f = jnp.zeros(x.shape, F32)
    for c in range(d_ff // ff_chunk):
        cols = slice(c * ff_chunk, (c + 1) * ff_chunk)
        hid = jnp.dot(h, wu_ref[:, cols], preferred_element_type=F32) + bu_ref[:, cols]
        hid = jnp.maximum(hid, 0.0)
        f = f + jnp.dot((hid * hid).astype(BF16), wd_ref[cols, :], preferred_element_type=F32)
    f = f + bd_ref[...]
    y_ref[...] = _ln(alpha * x + g2_ref[...] * f, lng_ref[...], lnb_ref[...])


def _mlp_call(l, x2, mods, per_row, tiles_per_batch, tm, w_up, b_up, w_down, b_down, ln_g, ln_b,
              alpha):
    n, d = x2.shape
    d_ff = w_up.shape[2]
    row = pl.BlockSpec((tm, d), lambda i: (i, 0))
    return pl.pallas_call(
        functools.partial(_mlp_kernel, alpha=alpha, ff_chunk=1024),
        grid=(n // tm,),
        in_specs=[
            row,
            _mod_spec(l, per_row, tm, d, tiles_per_batch, 3),
            _mod_spec(l, per_row, tm, d, tiles_per_batch, 4),
            _mod_spec(l, per_row, tm, d, tiles_per_batch, 5),
            _layer_spec(l, (d, d_ff)), _layer_spec(l, (1, d_ff)),
            _layer_spec(l, (d_ff, d)), _layer_spec(l, (1, d)),
            _layer_spec(l, (1, d)), _layer_spec(l, (1, d)),
        ],
        out_specs=row,
        out_shape=jax.ShapeDtypeStruct((n, d), F32),
        compiler_params=pltpu.CompilerParams(
            dimension_semantics=("arbitrary",), vmem_limit_bytes=VMEM_LIMIT),
    )(x2, mods, mods, mods, w_up, b_up, w_down, b_down, ln_g, ln_b)


def kernel(x_prompt, x_sample, cache_k, cache_v, page_table, c_prompt, c_sample, w_ada, b_ada, w_in, b_sb, ln_v_g, ln_v_b, w_s, b_s, w_a, w_b, w_o, ln1_g, ln1_b, w_up, b_up, w_down, b_down, ln2_g, ln2_b):
    batch, seq, d = x_prompt.shape
    db, t_new, _ = x_sample.shape
    depth, n_pool, page, n_heads, hd = cache_k.shape
    assert (n_heads, hd) == (H_B, HD_B) and page == CHUNK
    d_a = ln_v_g.shape[1]
    d_b = n_heads * hd
    n_pages = page_table.shape[1]
    alpha = (2 * depth) ** 0.25
    n_s = db * t_new
    assert CHUNK % t_new == 0 and n_s % CHUNK == 0

    n_c = batch + db
    pad = (-n_c) % 8
    c_all = jnp.concatenate([c_prompt, c_sample, jnp.zeros((pad, d), F32)], axis=0)
    mods = _mods_call(c_all, w_ada, b_ada)

    cache_k2 = jnp.transpose(cache_k, (0, 1, 3, 4, 2)).reshape(depth * n_pool, d_b, page)
    cache_v2 = jnp.transpose(cache_v, (0, 1, 3, 4, 2)).reshape(depth * n_pool, d_b, page)
    table = page_table.reshape(-1).astype(jnp.int32)

    tm_p = 512
    tm_s = CHUNK
    tpb = seq // tm_p
    n_open = CHUNK // t_new

    w_in16, w_a16, w_b16, w_o16 = (w.astype(BF16) for w in (w_in, w_a, w_b, w_o))
    w_up16, w_down16 = w_up.astype(BF16), w_down.astype(BF16)
    row3 = lambda p: p.reshape(depth, 1, p.shape[1])
    lnv_g, lnv_b, ln1g, ln1b, ln2g, ln2b = map(row3, (ln_v_g, ln_v_b, ln1_g, ln1_b, ln2_g, ln2_b))
    b_up3, b_down3 = row3(b_up), row3(b_down)
    mods_p = mods[:, :batch].reshape(depth, batch, 6, 1, d)
    mods_s = jnp.repeat(mods[:, batch:n_c], t_new, axis=1)
    bcol_p = jnp.swapaxes(b_s, 1, 2)
    eye = jnp.eye(n_open, dtype=F32)
    mix_s = (eye[None, None, :, None, :, None]
             * w_s[:, :, None, :t_new, None, :t_new]).reshape(depth, G_A, CHUNK, CHUNK)
    bcol_s = jnp.tile(bcol_p[:, :t_new], (1, n_open, 1))
    bias_col = jnp.repeat(b_sb, t_new, axis=1).reshape(depth, n_heads * t_new, 1)

    yp = x_prompt.reshape(batch * seq, d)
    ys = x_sample.reshape(n_s, d)
    kv_p = ()
    gv_p, k_s, v_s, gv_s = [], [], [], []
    for l in range(depth):
        u_p, vn_p, q_p, *kv_p, ga_p, gb_p = _inproj_call(
            l, yp, mods_p, False, tpb, tm_p, w_in16, lnv_g, lnv_b, d_a, d_b, tuple(kv_p))
        u_s, vn_s, q_s, k_sl, v_sl, ga_s, gb_s = _inproj_call(
            l, ys, mods_s, True, 1, tm_s, w_in16, lnv_g, lnv_b, d_a, d_b, None)

        q_rep = jnp.tile(q_s.reshape(db, t_new, d_b), (1, n_heads, 1))
        o_p, o_s = _attn_call(l, q_p, kv_p[0], kv_p[1], b_sb[l], batch, seq, q_rep, k_sl, v_sl,
                              bias_col[l], cache_k2, cache_v2, table + l * n_pool)

        x1 = _post_call(l, yp, o_p, u_p, vn_p, ga_p, gb_p, mods_p, False, tpb, tm_p, w_s, bcol_p,
                        w_a16, w_b16, w_o16, ln1g, ln1b, alpha)
        yp = _mlp_call(l, x1, mods_p, False, tpb, tm_p, w_up16, b_up3, w_down16, b_down3,
                       ln2g, ln2b, alpha)
        x1 = _post_call(l, ys, o_s.reshape(n_s, d_b), u_s, vn_s, ga_s, gb_s, mods_s, True, 1, tm_s,
                        mix_s, bcol_s, w_a16, w_b16, w_o16, ln1g, ln1b, alpha)
        ys = _mlp_call(l, x1, mods_s, True, 1, tm_s, w_up16, b_up3, w_down16, b_down3,
                       ln2g, ln2b, alpha)

        gv_p.append(vn_p.reshape(batch, seq, d_a)[:, seq - CHUNK:])
        k_s.append(k_sl.reshape(db, t_new, n_heads, hd))
        v_s.append(v_sl.reshape(db, t_new, n_heads, hd))
        gv_s.append(vn_s.reshape(db, t_new, d_a))

    to_rows = lambda x: jnp.transpose(x.reshape(depth, batch, n_heads, hd, seq), (0, 1, 4, 2, 3))
    return (yp.reshape(batch, seq, d), ys.reshape(db, t_new, d),
            to_rows(kv_p[0]), to_rows(kv_p[1]), jnp.stack(gv_p),
            jnp.stack(k_s), jnp.stack(v_s), jnp.stack(gv_s))
```

```python
import functools

import jax
import jax.numpy as jnp
from jax import lax
from jax.experimental import pallas as pl
from jax.experimental.pallas import tpu as pltpu

F32 = jnp.float32
BF16 = jnp.bfloat16

CHUNK = 128
G_A = 4
H_B = 8
HD_B = 64
LN_EPS = 1e-5
LOG2E = 1.4426950408889634
LANES = 128
ATT_TQ = 512
ATT_TK = 512
ATT_SUB = 256
SAMPLE_GROUP = 8
VMEM_LIMIT = 56 * 1024 * 1024


def _ln(x, g, b):
    mu = jnp.mean(x, axis=-1, keepdims=True)
    xc = x - mu
    var = jnp.mean(xc * xc, axis=-1, keepdims=True)
    return xc * lax.rsqrt(var + LN_EPS) * g + b


def _gelu(x):
    return 0.5 * x * (1.0 + lax.erf(x * (2.0 ** -0.5)))


def _sigmoid(x):
    return 1.0 / (1.0 + jnp.exp(-x))


def _div_pow2(x, n):
    assert n & (n - 1) == 0
    return x >> (n.bit_length() - 1)


def _mods_kernel(c_ref, w_ref, b_ref, o_ref):
    c = c_ref[...]
    s = c * _sigmoid(c)
    o_ref[...] = jnp.dot(s.astype(BF16), w_ref[...].astype(BF16),
                         preferred_element_type=F32) + b_ref[...]


def _mods_call(c_all, w_ada, b_ada):
    depth, d, n6 = w_ada.shape
    rows = c_all.shape[0]
    tn = 1536
    return pl.pallas_call(
        _mods_kernel,
        grid=(depth, n6 // tn),
        in_specs=[
            pl.BlockSpec((rows, d), lambda l, j: (0, 0)),
            pl.BlockSpec((None, d, tn), lambda l, j: (l, 0, j)),
            pl.BlockSpec((None, 1, tn), lambda l, j: (l, 0, j)),
        ],
        out_specs=pl.BlockSpec((None, rows, tn), lambda l, j: (l, 0, j)),
        out_shape=jax.ShapeDtypeStruct((depth, rows, n6), F32),
        compiler_params=pltpu.CompilerParams(
            dimension_semantics=("arbitrary", "arbitrary"), vmem_limit_bytes=VMEM_LIMIT),
    )(c_all, w_ada, b_ada.reshape(depth, 1, n6))


def _mod_spec(l, per_row, tm, d, tiles_per_batch, k):
    if per_row:
        return pl.BlockSpec((None, tm, d), lambda i: (l, i, k))
    return pl.BlockSpec((None, None, None, 1, d), lambda i: (l, i // tiles_per_batch, k, 0, 0))


def _layer_spec(l, shape):
    nd = len(shape)
    return pl.BlockSpec((None,) + tuple(shape), lambda i: (l,) + (0,) * nd)


def _inproj_kernel(x_ref, sh_ref, sc_ref, w_ref, lng_ref, lnb_ref, *rest,
                   d_a, d_b, d_model, kv_t):
    u_ref, vn_ref, q_ref, k_ref, v_ref, ga_ref, gb_ref = rest[-7:]
    h = (x_ref[...] * (1.0 + sc_ref[...]) + sh_ref[...]).astype(BF16)

    def seg(a, n):
        return jnp.dot(h, w_ref[:, a:a + n], preferred_element_type=F32)

    o = 0
    u_ref[...] = _gelu(seg(o, d_a)).astype(u_ref.dtype)
    o += d_a
    vn_ref[...] = _ln(_gelu(seg(o, d_a)), lng_ref[...], lnb_ref[...])
    o += d_a
    q_ref[...] = (seg(o, d_b) * (HD_B ** -0.5 * LOG2E)).astype(q_ref.dtype)
    o += d_b
    if kv_t:
        k_ref[...] = jnp.broadcast_to(seg(o, d_b).T, k_ref.shape)
        v_ref[...] = jnp.broadcast_to(seg(o + d_b, d_b).T, v_ref.shape)
    else:
        k_ref[...] = seg(o, d_b)
        v_ref[...] = seg(o + d_b, d_b)
    o += 2 * d_b
    ga_ref[...] = _sigmoid(seg(o, d_model)).astype(ga_ref.dtype)
    o += d_model
    gb_ref[...] = _sigmoid(seg(o, d_model)).astype(gb_ref.dtype)


def _inproj_call(l, x2, mods, per_row, tiles_per_batch, tm, w_in, ln_g, ln_b, d_a, d_b, kv_prev):
    n, d = x2.shape
    depth, _, d_in = w_in.shape
    row = lambda c: pl.BlockSpec((tm, c), lambda i: (i, 0))
    if kv_prev is not None:
        tpb = tiles_per_batch
        kv_shape = jax.ShapeDtypeStruct((depth, n // (tpb * tm), d_b, tpb * tm), F32)
        if kv_prev:
            kv_spec = pl.BlockSpec((None, None, d_b, tm), lambda i: (l, i // tpb, 0, i % tpb))
        else:
            kv_spec = pl.BlockSpec((depth, None, d_b, tm), lambda i: (0, i // tpb, 0, i % tpb))
    else:
        kv_prev = ()
        kv_shape = jax.ShapeDtypeStruct((n, d_b), F32)
        kv_spec = row(d_b)
    n_in = 6
    out_shape = (
        jax.ShapeDtypeStruct((n, d_a), BF16),
        jax.ShapeDtypeStruct((n, d_a), F32),
        jax.ShapeDtypeStruct((n, d_b), BF16),
        kv_shape,
        kv_shape,
        jax.ShapeDtypeStruct((n, d), BF16),
        jax.ShapeDtypeStruct((n, d), BF16),
    )
    return pl.pallas_call(
        functools.partial(_inproj_kernel, d_a=d_a, d_b=d_b, d_model=d, kv_t=kv_shape.ndim == 4),
        grid=(n // tm,),
        in_specs=[
            row(d),
            _mod_spec(l, per_row, tm, d, tiles_per_batch, 0),
            _mod_spec(l, per_row, tm, d, tiles_per_batch, 1),
            _layer_spec(l, (d, d_in)),
            _layer_spec(l, (1, d_a)),
            _layer_spec(l, (1, d_a)),
        ] + [pl.BlockSpec(memory_space=pl.ANY)] * len(kv_prev),
        out_specs=(row(d_a), row(d_a), row(d_b), kv_spec, kv_spec, row(d), row(d)),
        out_shape=out_shape,
        input_output_aliases={n_in + a: 3 + a for a in range(len(kv_prev))},
        compiler_params=pltpu.CompilerParams(
            dimension_semantics=("arbitrary",), vmem_limit_bytes=VMEM_LIMIT),
    )(x2, mods, mods, w_in, ln_g, ln_b, *kv_prev)


def _sb_softplus(z, valid):
    sign = jnp.uint32(0x80000000)
    neg_abs = lax.bitcast_convert_type(lax.bitcast_convert_type(z, jnp.uint32) | sign, F32)
    sp = jnp.maximum(z, 0.0) + jnp.log(1.0 + jnp.exp2(neg_abs)) * LOG2E
    if valid is not None:
        sp = jnp.where(valid, sp, 0.0)
    return sp


def _sb_weights(own, excl, carry, valid):
    a = jnp.exp2(own - excl - carry)
    if valid is not None:
        a = jnp.where(valid, a, 0.0)
    return a


def _sb_tile(z, ls, carry, valid):
    sp = _sb_softplus(z, valid)
    excl = jnp.dot(sp.astype(BF16), ls, preferred_element_type=F32)
    a = _sb_weights(z - sp, excl, carry, valid)
    return a, carry + excl[:, :1] + sp[:, :1]


def _prompt_step(i, j, bias_ref, q_ref, k_ref, v_ref, ls_ref, o_ref, carry_scr, acc_scr):
    tq, tk = q_ref.shape[0], k_ref.shape[1]
    sub_k = ls_ref.shape[0]
    n_pairs = q_ref.shape[1] // LANES

    @pl.when(j == i)
    def _():
        carry_scr[...] = jnp.zeros_like(carry_scr)
        acc_scr[...] = jnp.zeros_like(acc_scr)

    lane = lax.broadcasted_iota(jnp.int32, (1, LANES), 1)
    feat = lax.broadcasted_iota(jnp.int32, (LANES, 1), 0)
    ls = ls_ref[...]

    def sweep(masked):
        units = [(sub, h) for sub in reversed(range(tk // sub_k)) for h in range(2 * n_pairs)]
        st = {u: dict() for u in units}
        valid, k16, v16, q16 = {}, {}, {}, {}

        def tri(sub):
            if not masked:
                return None
            if sub not in valid:
                row = lax.broadcasted_iota(jnp.int32, (tq, sub_k), 0)
                col = lax.broadcasted_iota(jnp.int32, (tq, sub_k), 1)
                valid[sub] = col + sub * sub_k < row
            return valid[sub]

        def cols(h):
            return slice((h // 2) * LANES, (h // 2 + 1) * LANES)

        def pair_bf16(cache, ref, u):
            sub, h = u
            if (sub, h // 2) not in cache:
                cache[sub, h // 2] = ref[cols(h), sub * sub_k:(sub + 1) * sub_k].astype(BF16)
            return cache[sub, h // 2]

        def logits(u):
            h = u[1]
            x0 = HD_B if h % 2 == 0 else 0
            if h not in q16:
                q_m = (lane < HD_B) if h % 2 == 0 else (lane >= HD_B)
                b2 = jnp.full((1, LANES), bias_ref[h] * LOG2E, F32)
                b_hi = b2.astype(BF16).astype(F32)
                b_row = jnp.where(lane == x0, b_hi, jnp.where(lane == x0 + 1, b2 - b_hi, 0.0))
                qp = q_ref[:, cols(h)]
                q16[h] = jnp.where(q_m, qp, jnp.broadcast_to(b_row.astype(BF16), qp.shape))
            kp = pair_bf16(k16, k_ref, u)
            ke = jnp.where((feat == x0) | (feat == x0 + 1), jnp.ones_like(kp), kp)
            st[u]["z"] = jnp.dot(q16[h], ke, preferred_element_type=F32)

        def softplus(u):
            z = st[u].pop("z")
            sp = _sb_softplus(z, tri(u[0]))
            st[u]["own"] = z - sp
            st[u]["sp16"] = sp.astype(BF16)
            st[u]["sp0"] = sp[:, :1]

        def suffix(u):
            st[u]["excl"] = jnp.dot(st[u].pop("sp16"), ls, preferred_element_type=F32)

        def weights(u):
            excl = st[u].pop("excl")
            carry = carry_scr[u[1]]
            st[u]["a16"] = _sb_weights(st[u].pop("own"), excl, carry, tri(u[0])).astype(BF16)
            carry_scr[u[1]] = carry + excl[:, :1] + st[u].pop("sp0")

        def values(u):
            h = u[1]
            f_m = (feat < HD_B) if h % 2 == 0 else (feat >= HD_B)
            vp = pair_bf16(v16, v_ref, u)
            ve = jnp.where(f_m, vp, jnp.zeros_like(vp))
            acc_scr[:, cols(h)] += lax.dot_general(st[u].pop("a16"), ve, (((1,), (1,)), ((), ())),
                                                   preferred_element_type=F32)

        stages = (logits, softplus, suffix, weights, values)
        for t in range(len(units) + len(stages) - 1):
            for s in reversed(range(len(stages))):
                if 0 <= t - s < len(units):
                    stages[s](units[t - s])

    @pl.when(j == i)
    def _():
        sweep(True)

    @pl.when(j != i)
    def _():
        sweep(False)

    @pl.when(j == 0)
    def _():
        o_ref[...] = acc_scr[...].astype(o_ref.dtype)


def _strict_lower(n):
    r = lax.broadcasted_iota(jnp.int32, (n, n), 0)
    c = lax.broadcasted_iota(jnp.int32, (n, n), 1)
    return (r > c).astype(BF16)


_NT = (((1,), (1,)), ((), ()))


def _sample_queries(q_ref, bias_ref, t_new):
    rows, d_b = q_ref.shape
    r_i = lax.broadcasted_iota(jnp.int32, (rows, d_b), 0)
    l_i = lax.broadcasted_iota(jnp.int32, (rows, d_b), 1)
    head_mask = _div_pow2(r_i, t_new) == _div_pow2(l_i, HD_B)
    qbd = jnp.where(head_mask, q_ref[...], jnp.zeros_like(q_ref[...]))
    return qbd, bias_ref[...] * LOG2E, head_mask


def _sample_begin(b, t_new, q_ref, kn_ref, vn_ref, bias_ref, lsn_ref, carry_scr, acc_scr):
    rows, n_new = q_ref.shape[0], kn_ref.shape[0]
    qbd, bias, _ = _sample_queries(q_ref, bias_ref, t_new)
    rr = lax.broadcasted_iota(jnp.int32, (rows, n_new), 0)
    cc = lax.broadcasted_iota(jnp.int32, (rows, n_new), 1)
    own = (_div_pow2(cc, t_new) == b) & ((cc & (t_new - 1)) < (rr & (t_new - 1)))
    z = lax.dot_general(qbd, kn_ref[...].astype(BF16), _NT, preferred_element_type=F32) + bias
    a, carry = _sb_tile(z, lsn_ref[...], jnp.zeros((rows, 1), F32), own)
    carry_scr[...] = carry
    acc_scr[...] = jnp.dot(a.astype(BF16), vn_ref[...].astype(BF16), preferred_element_type=F32)


def _sample_page_stages(t_new, q_ref, bias_ref, ls_ref, k_pages, v_pages, carry_scr, acc_scr):
    pages = len(k_pages)
    rows = q_ref.shape[0]
    st = {}

    def logits():
        st["q"], bias, _ = _sample_queries(q_ref, bias_ref, t_new)
        st["z"] = [jnp.dot(st["q"], k_pages[r][...].astype(BF16), preferred_element_type=F32) + bias
                   for r in range(pages)]

    def softplus():
        st["sp"] = [_sb_softplus(z, None) for z in st["z"]]
        st["own"] = [z - sp for z, sp in zip(st.pop("z"), st["sp"])]

    def suffix():
        st["excl"] = jnp.dot(jnp.concatenate(st["sp"], axis=0).astype(BF16), ls_ref[...],
                             preferred_element_type=F32)

    def weights():
        carry = carry_scr[...]
        st["a"] = []
        for r in range(pages):
            excl = st["excl"][r * rows:(r + 1) * rows]
            st["a"].append(_sb_weights(st["own"][r], excl, carry, None).astype(BF16))
            carry = carry + excl[:, :1] + st["sp"][r][:, :1]
        carry_scr[...] = carry

    def values():
        acc = jnp.zeros(acc_scr.shape, F32)
        for r in range(pages):
            acc = acc + lax.dot_general(st["a"][r], v_pages[r][...].astype(BF16), _NT,
                                        preferred_element_type=F32)
        acc_scr[...] += acc

    return [logits, softplus, suffix, weights, values]


def _sample_end(t_new, q_ref, bias_ref, o_ref, acc_scr):
    rows = q_ref.shape[0]
    _, _, head_mask = _sample_queries(q_ref, bias_ref, t_new)
    accm = jnp.where(head_mask, acc_scr[...], 0.0)
    out = accm[0:t_new, :]
    for h in range(1, rows // t_new):
        out = out + accm[h * t_new:(h + 1) * t_new, :]
    o_ref[...] = out.astype(o_ref.dtype)


def _page_copies(page_ref, step, slot, ck_ref, cv_ref, kbuf, vbuf, sem):
    pages = kbuf.shape[1]
    copies = []
    for r in range(pages):
        pid = page_ref[step * pages + r]
        copies.append(pltpu.make_async_copy(ck_ref.at[pid], kbuf.at[slot, r], sem.at[0, slot]))
        copies.append(pltpu.make_async_copy(cv_ref.at[pid], vbuf.at[slot, r], sem.at[1, slot]))
    return copies


def _attn_kernel(qi_ref, kj_ref, seq_ref, page_ref, bias_ref, q_ref, k_ref, v_ref, ls_ref,
                 qs_ref, kn_ref, vn_ref, bcol_ref, lsn_ref, lsp_ref, ck_ref, cv_ref,
                 o_ref, os_ref, carry_scr, acc_scr, carry_s, acc_s, kbuf, vbuf, sem,
                 *, t_new, seq_steps, sample_steps):
    pages = kbuf.shape[1]
    n = pl.program_id(1)
    lin = pl.program_id(0) * pl.num_programs(1) + n
    active = lin < sample_steps
    s = lax.rem(lin, seq_steps)
    slot = lax.rem(lin, 2)

    def copies(step, into):
        return _page_copies(page_ref, step, into, ck_ref, cv_ref, kbuf, vbuf, sem)

    @pl.when(lin == 0)
    def _():
        for c in copies(0, 0):
            c.start()

    @pl.when(lin + 1 < sample_steps)
    def _():
        for c in copies(lin + 1, 1 - slot):
            c.start()

    @pl.when(active & (s == 0))
    def _():
        _sample_begin(seq_ref[lin], t_new, qs_ref, kn_ref, vn_ref, bcol_ref, lsn_ref,
                      carry_s, acc_s)

    _prompt_step(qi_ref[n], kj_ref[n], bias_ref, q_ref, k_ref, v_ref, ls_ref, o_ref,
                 carry_scr, acc_scr)

    @pl.when(active)
    def _():
        for c in copies(lin, slot):
            c.wait()
        k_pages = [kbuf.at[slot, r] for r in range(pages)]
        v_pages = [vbuf.at[slot, r] for r in range(pages)]
        group = min(pages, SAMPLE_GROUP)
        chains = [_sample_page_stages(t_new, qs_ref, bcol_ref, lsp_ref, k_pages[g:g + group],
                                      v_pages[g:g + group], carry_s, acc_s)
                  for g in range(0, pages, group)]
        n_stage = len(chains[0])
        for t in range(len(chains) + n_stage - 1):
            for st in reversed(range(n_stage)):
                if 0 <= t - st < len(chains):
                    chains[t - st][st]()

    @pl.when(active & (s == seq_steps - 1))
    def _():
        _sample_end(t_new, qs_ref, bcol_ref, os_ref, acc_s)


def _attn_call(l, q, k, v, bias, batch, seq, q_rep, k_new, v_new, bias_col, cache_k, cache_v,
               table):
    d_b = q.shape[1]
    assert ATT_TQ == ATT_TK and seq % ATT_TQ == 0 and ATT_TK % ATT_SUB == 0
    nq = seq // ATT_TQ
    qi = jnp.asarray([i for i in range(nq) for _ in range(i + 1)], jnp.int32)
    kj = jnp.asarray([j for i in range(nq) for j in range(i, -1, -1)], jnp.int32)
    n_steps = qi.shape[0]

    db, rows, _ = q_rep.shape
    t_new = rows // H_B
    n_new = k_new.shape[0]
    page = cache_k.shape[2]
    n_pages = table.shape[0] // db
    pages = min(p for p in range(1, n_pages + 1)
                if n_pages % p == 0 and db * (n_pages // p) <= batch * n_steps)
    seq_steps = n_pages // pages
    sample_steps = db * seq_steps

    lin = jnp.minimum(jnp.arange(batch * n_steps, dtype=jnp.int32), sample_steps - 1)
    seq_of = lin // seq_steps
    logical = n_pages - 1 - ((lin % seq_steps)[:, None] * pages
                             + jnp.arange(pages, dtype=jnp.int32)[None, :])
    page_of = table.reshape(db, n_pages)[seq_of[:, None], logical].reshape(-1)

    def per_seq(shape):
        return pl.BlockSpec((None,) + shape,
                            lambda b, n, qi, kj, sq, pg: (sq[b * n_steps + n], 0, 0))

    const = lambda shape: pl.BlockSpec(shape, lambda b, n, qi, kj, sq, pg: (0,) * len(shape))
    grid_spec = pltpu.PrefetchScalarGridSpec(
        num_scalar_prefetch=4,
        grid=(batch, n_steps),
        in_specs=[
            pl.BlockSpec(memory_space=pltpu.SMEM),
            pl.BlockSpec((ATT_TQ, d_b), lambda b, n, qi, kj, sq, pg: (b * nq + qi[n], 0)),
            pl.BlockSpec((None, None, d_b, ATT_TK), lambda b, n, qi, kj, sq, pg: (l, b, 0, kj[n])),
            pl.BlockSpec((None, None, d_b, ATT_TK), lambda b, n, qi, kj, sq, pg: (l, b, 0, kj[n])),
            const((ATT_SUB, ATT_SUB)),
            per_seq((rows, d_b)),
            const((n_new, d_b)), const((n_new, d_b)),
            const((rows, 1)), const((n_new, n_new)), const((page, page)),
            pl.BlockSpec(memory_space=pl.ANY), pl.BlockSpec(memory_space=pl.ANY),
        ],
        out_specs=(
            pl.BlockSpec((ATT_TQ, d_b), lambda b, n, qi, kj, sq, pg: (b * nq + qi[n], 0)),
            per_seq((t_new, d_b)),
        ),
        scratch_shapes=[
            pltpu.VMEM((H_B, ATT_TQ, 1), F32),
            pltpu.VMEM((ATT_TQ, d_b), F32),
            pltpu.VMEM((rows, 1), F32),
            pltpu.VMEM((rows, d_b), F32),
            pltpu.VMEM((2, pages, d_b, page), F32),
            pltpu.VMEM((2, pages, d_b, page), F32),
            pltpu.SemaphoreType.DMA((2, 2)),
        ],
    )
    return pl.pallas_call(
        functools.partial(_attn_kernel, t_new=t_new, seq_steps=seq_steps,
                          sample_steps=sample_steps),
        grid_spec=grid_spec,
        out_shape=(jax.ShapeDtypeStruct(q.shape, BF16),
                   jax.ShapeDtypeStruct((db, t_new, d_b), BF16)),
        compiler_params=pltpu.CompilerParams(
            dimension_semantics=("arbitrary", "arbitrary"), vmem_limit_bytes=VMEM_LIMIT),
    )(qi, kj, seq_of, page_of, bias, q, k, v, _strict_lower(ATT_SUB),
      q_rep, k_new, v_new, bias_col, _strict_lower(n_new), _strict_lower(page),
      cache_k, cache_v)


def _post_kernel(x_ref, o_ref, u_ref, vn_ref, ga_ref, gb_ref, g1_ref, mix_ref, bcol_ref,
                 wa_ref, wb_ref, wo_ref, lng_ref, lnb_ref, y_ref, ya_scr, *, alpha):
    tm = x_ref.shape[0]
    r = lax.broadcasted_iota(jnp.int32, (CHUNK, CHUNK), 0)
    c = lax.broadcasted_iota(jnp.int32, (CHUNK, CHUNK), 1)
    causal = r >= c
    cg = vn_ref.shape[1] // G_A
    for g in range(G_A):
        w = jnp.where(causal, mix_ref[g], 0.0).astype(BF16)
        bias = bcol_ref[:, g:g + 1]
        cols = slice(g * cg, (g + 1) * cg)
        for ch in range(tm // CHUNK):
            rows = slice(ch * CHUNK, (ch + 1) * CHUNK)
            mixed = jnp.dot(w, vn_ref[rows, cols].astype(BF16), preferred_element_type=F32) + bias
            ya_scr[rows, cols] = (u_ref[rows, cols].astype(F32) * mixed).astype(BF16)
    y_a = jnp.dot(ya_scr[...], wa_ref[...], preferred_element_type=F32)
    y_b = jnp.dot(o_ref[...], wb_ref[...], preferred_element_type=F32)
    merged = ga_ref[...].astype(F32) * y_a + gb_ref[...].astype(F32) * y_b
    t = jnp.dot(merged.astype(BF16), wo_ref[...], preferred_element_type=F32)
    y_ref[...] = _ln(alpha * x_ref[...] + g1_ref[...] * t, lng_ref[...], lnb_ref[...])


def _post_call(l, x2, o, u, vn, ga, gb, mods, per_row, tiles_per_batch, tm, mix, bcol,
               w_a, w_b, w_o, ln_g, ln_b, alpha):
    n, d = x2.shape
    d_a, d_b = u.shape[1], o.shape[1]
    row = lambda c: pl.BlockSpec((tm, c), lambda i: (i, 0))
    return pl.pallas_call(
        functools.partial(_post_kernel, alpha=alpha),
        grid=(n // tm,),
        in_specs=[
            row(d), row(d_b), row(d_a), row(d_a), row(d), row(d),
            _mod_spec(l, per_row, tm, d, tiles_per_batch, 2),
            _layer_spec(l, mix.shape[1:]), _layer_spec(l, bcol.shape[1:]),
            _layer_spec(l, w_a.shape[1:]), _layer_spec(l, w_b.shape[1:]),
            _layer_spec(l, w_o.shape[1:]),
            _layer_spec(l, (1, d)), _layer_spec(l, (1, d)),
        ],
        out_specs=row(d),
        out_shape=jax.ShapeDtypeStruct((n, d), F32),
        scratch_shapes=[pltpu.VMEM((tm, d_a), BF16)],
        compiler_params=pltpu.CompilerParams(
            dimension_semantics=("arbitrary",), vmem_limit_bytes=VMEM_LIMIT),
    )(x2, o, u, vn, ga, gb, mods, mix, bcol, w_a, w_b, w_o, ln_g, ln_b)


def _mlp_kernel(x_ref, sh_ref, sc_ref, g2_ref, wu_ref, bu_ref, wd_ref, bd_ref, lng_ref, lnb_ref,
                y_ref, *, alpha, ff_chunk):
    x = x_ref[...]
    h = (x * (1.0 + sc_ref[...]) + sh_ref[...]).astype(BF16)
    d_ff = wu_ref.shape[1]
    f = jnp.zeros(x.shape, F32)
    for c in range(d_ff // ff_chunk):
        cols = slice(c * ff_chunk, (c + 1) * ff_chunk)
        hid = jnp.dot(h, wu_ref[:, cols], preferred_element_type=F32) + bu_ref[:, cols]
        hid = jnp.maximum(hid, 0.0)
        f = f + jnp.dot((hid * hid).astype(BF16), wd_ref[cols, :], preferred_element_type=F32)
    f = f + bd_ref[...]
    y_ref[...] = _ln(alpha * x + g2_ref[...] * f, lng_ref[...], lnb_ref[...])


def _mlp_call(l, x2, mods, per_row, tiles_per_batch, tm, w_up, b_up, w_down, b_down, ln_g, ln_b,
              alpha):
    n, d = x2.shape
    d_ff = w_up.shape[2]
    row = pl.BlockSpec((tm, d), lambda i: (i, 0))
    return pl.pallas_call(
        functools.partial(_mlp_kernel, alpha=alpha, ff_chunk=1024),
        grid=(n // tm,),
        in_specs=[
            row,
            _mod_spec(l, per_row, tm, d, tiles_per_batch, 3),
            _mod_spec(l, per_row, tm, d, tiles_per_batch, 4),
            _mod_spec(l, per_row, tm, d, tiles_per_batch, 5),
            _layer_spec(l, (d, d_ff)), _layer_spec(l, (1, d_ff)),
            _layer_spec(l, (d_ff, d)), _layer_spec(l, (1, d)),
            _layer_spec(l, (1, d)), _layer_spec(l, (1, d)),
        ],
        out_specs=row,
        out_shape=jax.ShapeDtypeStruct((n, d), F32),
        compiler_params=pltpu.CompilerParams(
            dimension_semantics=("arbitrary",), vmem_limit_bytes=VMEM_LIMIT),
    )(x2, mods, mods, mods, w_up, b_up, w_down, b_down, ln_g, ln_b)


def kernel(x_prompt, x_sample, cache_k, cache_v, page_table, c_prompt, c_sample, w_ada, b_ada, w_in, b_sb, ln_v_g, ln_v_b, w_s, b_s, w_a, w_b, w_o, ln1_g, ln1_b, w_up, b_up, w_down, b_down, ln2_g, ln2_b):
    batch, seq, d = x_prompt.shape
    db, t_new, _ = x_sample.shape
    depth, n_pool, page, n_heads, hd = cache_k.shape
    assert (n_heads, hd) == (H_B, HD_B) and page == CHUNK
    d_a = ln_v_g.shape[1]
    d_b = n_heads * hd
    n_pages = page_table.shape[1]
    alpha = (2 * depth) ** 0.25
    n_s = db * t_new
    assert CHUNK % t_new == 0 and n_s % CHUNK == 0

    n_c = batch + db
    pad = (-n_c) % 8
    c_all = jnp.concatenate([c_prompt, c_sample, jnp.zeros((pad, d), F32)], axis=0)
    mods = _mods_call(c_all, w_ada, b_ada)

    cache_k2 = jnp.transpose(cache_k, (0, 1, 3, 4, 2)).reshape(depth * n_pool, d_b, page)
    cache_v2 = jnp.transpose(cache_v, (0, 1, 3, 4, 2)).reshape(depth * n_pool, d_b, page)
    table = page_table.reshape(-1).astype(jnp.int32)

    tm_p = 512
    tm_s = CHUNK
    tpb = seq // tm_p
    n_open = CHUNK // t_new

    w_in16, w_a16, w_b16, w_o16 = (w.astype(BF16) for w in (w_in, w_a, w_b, w_o))
    w_up16, w_down16 = w_up.astype(BF16), w_down.astype(BF16)
    row3 = lambda p: p.reshape(depth, 1, p.shape[1])
    lnv_g, lnv_b, ln1g, ln1b, ln2g, ln2b = map(row3, (ln_v_g, ln_v_b, ln1_g, ln1_b, ln2_g, ln2_b))
    b_up3, b_down3 = row3(b_up), row3(b_down)
    mods_p = mods[:, :batch].reshape(depth, batch, 6, 1, d)
    mods_s = jnp.repeat(mods[:, batch:n_c], t_new, axis=1)
    bcol_p = jnp.swapaxes(b_s, 1, 2)
    pos = jnp.arange(CHUNK, dtype=jnp.int32) // t_new
    mix_s = jnp.where(pos[:, None] == pos[None, :],
                      jnp.tile(w_s[:, :, :t_new, :t_new], (1, 1, n_open, n_open)), 0.0)
    bcol_s = jnp.tile(bcol_p[:, :t_new], (1, n_open, 1))
    bias_col = jnp.repeat(b_sb, t_new, axis=1).reshape(depth, n_heads * t_new, 1)

    yp = x_prompt.reshape(batch * seq, d)
    ys = x_sample.reshape(n_s, d)
    kv_p = ()
    gv_p, k_s, v_s, gv_s = [], [], [], []
    for l in range(depth):
        u_p, vn_p, q_p, *kv_p, ga_p, gb_p = _inproj_call(
            l, yp, mods_p, False, tpb, tm_p, w_in16, lnv_g, lnv_b, d_a, d_b, tuple(kv_p))
        u_s, vn_s, q_s, k_sl, v_sl, ga_s, gb_s = _inproj_call(
            l, ys, mods_s, True, 1, tm_s, w_in16, lnv_g, lnv_b, d_a, d_b, None)

        q_rep = jnp.tile(q_s.reshape(db, t_new, d_b), (1, n_heads, 1))
        o_p, o_s = _attn_call(l, q_p, kv_p[0], kv_p[1], b_sb[l], batch, seq, q_rep, k_sl, v_sl,
                              bias_col[l], cache_k2, cache_v2, table + l * n_pool)

        x1 = _post_call(l, yp, o_p, u_p, vn_p, ga_p, gb_p, mods_p, False, tpb, tm_p, w_s, bcol_p,
                        w_a16, w_b16, w_o16, ln1g, ln1b, alpha)
        yp = _mlp_call(l, x1, mods_p, False, tpb, tm_p, w_up16, b_up3, w_down16, b_down3,
                       ln2g, ln2b, alpha)
        x1 = _post_call(l, ys, o_s.reshape(n_s, d_b), u_s, vn_s, ga_s, gb_s, mods_s, True, 1, tm_s,
                        mix_s, bcol_s, w_a16, w_b16, w_o16, ln1g, ln1b, alpha)
        ys = _mlp_call(l, x1, mods_s, True, 1, tm_s, w_up16, b_up3, w_down16, b_down3,
                       ln2g, ln2b, alpha)

        gv_p.append(vn_p.reshape(batch, seq, d_a)[:, seq - CHUNK:])
        k_s.append(k_sl.reshape(db, t_new, n_heads, hd))
        v_s.append(v_sl.reshape(db, t_new, n_heads, hd))
        gv_s.append(vn_s.reshape(db, t_new, d_a))

    to_rows = lambda x: jnp.transpose(x.reshape(depth, batch, n_heads, hd, seq), (0, 1, 4, 2, 3))
    return (yp.reshape(batch, seq, d), ys.reshape(db, t_new, d),
            to_rows(kv_p[0]), to_rows(kv_p[1]), jnp.stack(gv_p),
            jnp.stack(k_s), jnp.stack(v_s), jnp.stack(gv_s))
```

```python
import functools

import jax
import jax.numpy as jnp
from jax import lax
from jax.experimental import pallas as pl
from jax.experimental.pallas import tpu as pltpu

F32 = jnp.float32
BF16 = jnp.bfloat16

CHUNK = 128
G_A = 4
H_B = 8
HD_B = 64
LN_EPS = 1e-5
LOG2E = 1.4426950408889634
LANES = 128
ATT_TQ = 512
ATT_TK = 512
ATT_SUB = 256
SAMPLE_GROUP = 8
VMEM_LIMIT = 56 * 1024 * 1024


def _ln(x, g, b):
    mu = jnp.mean(x, axis=-1, keepdims=True)
    xc = x - mu
    var = jnp.mean(xc * xc, axis=-1, keepdims=True)
    return xc * lax.rsqrt(var + LN_EPS) * g + b


def _gelu(x):
    return 0.5 * x * (1.0 + lax.erf(x * (2.0 ** -0.5)))


def _sigmoid(x):
    return 1.0 / (1.0 + jnp.exp(-x))


def _div_pow2(x, n):
    assert n & (n - 1) == 0
    return x >> (n.bit_length() - 1)


def _mods_kernel(c_ref, w_ref, b_ref, o_ref):
    c = c_ref[...]
    s = c * _sigmoid(c)
    o_ref[...] = jnp.dot(s.astype(BF16), w_ref[...].astype(BF16),
                         preferred_element_type=F32) + b_ref[...]


def _mods_call(c_all, w_ada, b_ada):
    depth, d, n6 = w_ada.shape
    rows = c_all.shape[0]
    tn = 1536
    return pl.pallas_call(
        _mods_kernel,
        grid=(depth, n6 // tn),
        in_specs=[
            pl.BlockSpec((rows, d), lambda l, j: (0, 0)),
            pl.BlockSpec((None, d, tn), lambda l, j: (l, 0, j)),
            pl.BlockSpec((None, 1, tn), lambda l, j: (l, 0, j)),
        ],
        out_specs=pl.BlockSpec((None, rows, tn), lambda l, j: (l, 0, j)),
        out_shape=jax.ShapeDtypeStruct((depth, rows, n6), F32),
        compiler_params=pltpu.CompilerParams(
            dimension_semantics=("arbitrary", "arbitrary"), vmem_limit_bytes=VMEM_LIMIT),
    )(c_all, w_ada, b_ada.reshape(depth, 1, n6))


def _mod_spec(l, per_row, tm, d, tiles_per_batch, k):
    if per_row:
        return pl.BlockSpec((None, tm, d), lambda i: (l, i, k))
    return pl.BlockSpec((None, None, None, 1, d), lambda i: (l, i // tiles_per_batch, k, 0, 0))


def _layer_spec(l, shape):
    nd = len(shape)
    return pl.BlockSpec((None,) + tuple(shape), lambda i: (l,) + (0,) * nd,
                        pipeline_mode=pl.Buffered(1))


def _inproj_kernel(x_ref, sh_ref, sc_ref, w_ref, lng_ref, lnb_ref, *rest,
                   d_a, d_b, d_model, kv_t):
    u_ref, vn_ref, q_ref, k_ref, v_ref, ga_ref, gb_ref = rest[-7:]
    h = (x_ref[...] * (1.0 + sc_ref[...]) + sh_ref[...]).astype(BF16)

    def seg(a, n):
        return jnp.dot(h, w_ref[:, a:a + n], preferred_element_type=F32)

    o = 0
    u_ref[...] = _gelu(seg(o, d_a)).astype(u_ref.dtype)
    o += d_a
    vn_ref[...] = _ln(_gelu(seg(o, d_a)), lng_ref[...], lnb_ref[...])
    o += d_a
    q_ref[...] = (seg(o, d_b) * (HD_B ** -0.5 * LOG2E)).astype(q_ref.dtype)
    o += d_b
    if kv_t:
        k_ref[...] = jnp.broadcast_to(seg(o, d_b).T, k_ref.shape)
        v_ref[...] = jnp.broadcast_to(seg(o + d_b, d_b).T, v_ref.shape)
    else:
        k_ref[...] = seg(o, d_b)
        v_ref[...] = seg(o + d_b, d_b)
    o += 2 * d_b
    ga_ref[...] = _sigmoid(seg(o, d_model)).astype(ga_ref.dtype)
    o += d_model
    gb_ref[...] = _sigmoid(seg(o, d_model)).astype(gb_ref.dtype)


def _inproj_call(l, x2, mods, per_row, tiles_per_batch, tm, w_in, ln_g, ln_b, d_a, d_b, kv_prev):
    n, d = x2.shape
    depth, _, d_in = w_in.shape
    row = lambda c: pl.BlockSpec((tm, c), lambda i: (i, 0))
    if kv_prev is not None:
        tpb = tiles_per_batch
        kv_shape = jax.ShapeDtypeStruct((depth, n // (tpb * tm), d_b, tpb * tm), F32)
        if kv_prev:
            kv_spec = pl.BlockSpec((None, None, d_b, tm), lambda i: (l, i // tpb, 0, i % tpb))
        else:
            kv_spec = pl.BlockSpec((depth, None, d_b, tm), lambda i: (0, i // tpb, 0, i % tpb))
    else:
        kv_prev = ()
        kv_shape = jax.ShapeDtypeStruct((n, d_b), F32)
        kv_spec = row(d_b)
    n_in = 6
    out_shape = (
        jax.ShapeDtypeStruct((n, d_a), BF16),
        jax.ShapeDtypeStruct((n, d_a), F32),
        jax.ShapeDtypeStruct((n, d_b), BF16),
        kv_shape,
        kv_shape,
        jax.ShapeDtypeStruct((n, d), BF16),
        jax.ShapeDtypeStruct((n, d), BF16),
    )
    return pl.pallas_call(
        functools.partial(_inproj_kernel, d_a=d_a, d_b=d_b, d_model=d, kv_t=kv_shape.ndim == 4),
        grid=(n // tm,),
        in_specs=[
            row(d),
            _mod_spec(l, per_row, tm, d, tiles_per_batch, 0),
            _mod_spec(l, per_row, tm, d, tiles_per_batch, 1),
            _layer_spec(l, (d, d_in)),
            _layer_spec(l, (1, d_a)),
            _layer_spec(l, (1, d_a)),
        ] + [pl.BlockSpec(memory_space=pl.ANY)] * len(kv_prev),
        out_specs=(row(d_a), row(d_a), row(d_b), kv_spec, kv_spec, row(d), row(d)),
        out_shape=out_shape,
        input_output_aliases={n_in + a: 3 + a for a in range(len(kv_prev))},
        compiler_params=pltpu.CompilerParams(
            dimension_semantics=("arbitrary",), vmem_limit_bytes=VMEM_LIMIT),
    )(x2, mods, mods, w_in, ln_g, ln_b, *kv_prev)


def _sb_softplus(z, valid):
    sign = jnp.uint32(0x80000000)
    neg_abs = lax.bitcast_convert_type(lax.bitcast_convert_type(z, jnp.uint32) | sign, F32)
    sp = jnp.maximum(z, 0.0) + jnp.log(1.0 + jnp.exp2(neg_abs)) * LOG2E
    if valid is not None:
        sp = jnp.where(valid, sp, 0.0)
    return sp


def _sb_weights(own, excl, carry, valid):
    a = jnp.exp2(own - excl - carry)
    if valid is not None:
        a = jnp.where(valid, a, 0.0)
    return a


def _sb_tile(z, ls, carry, valid):
    sp = _sb_softplus(z, valid)
    excl = jnp.dot(sp.astype(BF16), ls, preferred_element_type=F32)
    a = _sb_weights(z - sp, excl, carry, valid)
    return a, carry + excl[:, :1] + sp[:, :1]


def _prompt_step(i, j, bias_ref, q_ref, k_ref, v_ref, ls_ref, o_ref, carry_scr, acc_scr):
    tq, tk = q_ref.shape[0], k_ref.shape[1]
    sub_k = ls_ref.shape[0]
    n_pairs = q_ref.shape[1] // LANES

    @pl.when(j == i)
    def _():
        carry_scr[...] = jnp.zeros_like(carry_scr)
        acc_scr[...] = jnp.zeros_like(acc_scr)

    lane = lax.broadcasted_iota(jnp.int32, (1, LANES), 1)
    feat = lax.broadcasted_iota(jnp.int32, (LANES, 1), 0)
    ls = ls_ref[...]

    def sweep(masked):
        units = [(sub, h) for sub in reversed(range(tk // sub_k)) for h in range(2 * n_pairs)]
        st = {u: dict() for u in units}
        valid, k16, v16, q16 = {}, {}, {}, {}

        def tri(sub):
            if not masked:
                return None
            if sub not in valid:
                row = lax.broadcasted_iota(jnp.int32, (tq, sub_k), 0)
                col = lax.broadcasted_iota(jnp.int32, (tq, sub_k), 1)
                valid[sub] = col + sub * sub_k < row
            return valid[sub]

        def cols(h):
            return slice((h // 2) * LANES, (h // 2 + 1) * LANES)

        def pair_bf16(cache, ref, u):
            sub, h = u
            if (sub, h // 2) not in cache:
                cache[sub, h // 2] = ref[cols(h), sub * sub_k:(sub + 1) * sub_k].astype(BF16)
            return cache[sub, h // 2]

        def logits(u):
            h = u[1]
            x0 = HD_B if h % 2 == 0 else 0
            if h not in q16:
                q_m = (lane < HD_B) if h % 2 == 0 else (lane >= HD_B)
                b2 = jnp.full((1, LANES), bias_ref[h] * LOG2E, F32)
                b_hi = b2.astype(BF16).astype(F32)
                b_row = jnp.where(lane == x0, b_hi, jnp.where(lane == x0 + 1, b2 - b_hi, 0.0))
                qp = q_ref[:, cols(h)]
                q16[h] = jnp.where(q_m, qp, jnp.broadcast_to(b_row.astype(BF16), qp.shape))
            kp = pair_bf16(k16, k_ref, u)
            ke = jnp.where((feat == x0) | (feat == x0 + 1), jnp.ones_like(kp), kp)
            st[u]["z"] = jnp.dot(q16[h], ke, preferred_element_type=F32)

        def softplus(u):
            z = st[u].pop("z")
            sp = _sb_softplus(z, tri(u[0]))
            st[u]["own"] = z - sp
            st[u]["sp16"] = sp.astype(BF16)
            st[u]["sp0"] = sp[:, :1]

        def suffix(u):
            st[u]["excl"] = jnp.dot(st[u].pop("sp16"), ls, preferred_element_type=F32)

        def weights(u):
            excl = st[u].pop("excl")
            carry = carry_scr[u[1]]
            st[u]["a16"] = _sb_weights(st[u].pop("own"), excl, carry, tri(u[0])).astype(BF16)
            carry_scr[u[1]] = carry + excl[:, :1] + st[u].pop("sp0")

        def values(u):
            h = u[1]
            f_m = (feat < HD_B) if h % 2 == 0 else (feat >= HD_B)
            vp = pair_bf16(v16, v_ref, u)
            ve = jnp.where(f_m, vp, jnp.zeros_like(vp))
            acc_scr[:, cols(h)] += lax.dot_general(st[u].pop("a16"), ve, (((1,), (1,)), ((), ())),
                                                   preferred_element_type=F32)

        stages = (logits, softplus, suffix, weights, values)
        for t in range(len(units) + len(stages) - 1):
            for s in reversed(range(len(stages))):
                if 0 <= t - s < len(units):
                    stages[s](units[t - s])

    @pl.when(j == i)
    def _():
        sweep(True)

    @pl.when(j != i)
    def _():
        sweep(False)

    @pl.when(j == 0)
    def _():
        o_ref[...] = acc_scr[...].astype(o_ref.dtype)


def _strict_lower(n):
    r = lax.broadcasted_iota(jnp.int32, (n, n), 0)
    c = lax.broadcasted_iota(jnp.int32, (n, n), 1)
    return (r > c).astype(BF16)


_NT = (((1,), (1,)), ((), ()))


def _sample_queries(q_ref, bias_ref, t_new):
    rows, d_b = q_ref.shape
    r_i = lax.broadcasted_iota(jnp.int32, (rows, d_b), 0)
    l_i = lax.broadcasted_iota(jnp.int32, (rows, d_b), 1)
    head_mask = _div_pow2(r_i, t_new) == _div_pow2(l_i, HD_B)
    qbd = jnp.where(head_mask, q_ref[...], jnp.zeros_like(q_ref[...]))
    return qbd, bias_ref[...] * LOG2E, head_mask


def _sample_begin(b, t_new, q_ref, kn_ref, vn_ref, bias_ref, lsn_ref, carry_scr, acc_scr):
    rows, n_new = q_ref.shape[0], kn_ref.shape[0]
    qbd, bias, _ = _sample_queries(q_ref, bias_ref, t_new)
    rr = lax.broadcasted_iota(jnp.int32, (rows, n_new), 0)
    cc = lax.broadcasted_iota(jnp.int32, (rows, n_new), 1)
    own = (_div_pow2(cc, t_new) == b) & ((cc & (t_new - 1)) < (rr & (t_new - 1)))
    z = lax.dot_general(qbd, kn_ref[...].astype(BF16), _NT, preferred_element_type=F32) + bias
    a, carry = _sb_tile(z, lsn_ref[...], jnp.zeros((rows, 1), F32), own)
    carry_scr[...] = carry
    acc_scr[...] = jnp.dot(a.astype(BF16), vn_ref[...].astype(BF16), preferred_element_type=F32)


def _sample_page_stages(t_new, q_ref, bias_ref, ls_ref, k_pages, v_pages, carry_scr, acc_scr):
    pages = len(k_pages)
    rows = q_ref.shape[0]
    st = {}

    def logits():
        st["q"], bias, _ = _sample_queries(q_ref, bias_ref, t_new)
        st["z"] = [jnp.dot(st["q"], k_pages[r][...].astype(BF16), preferred_element_type=F32) + bias
                   for r in range(pages)]

    def softplus():
        st["sp"] = [_sb_softplus(z, None) for z in st["z"]]
        st["own"] = [z - sp for z, sp in zip(st.pop("z"), st["sp"])]

    def suffix():
        st["excl"] = jnp.dot(jnp.concatenate(st["sp"], axis=0).astype(BF16), ls_ref[...],
                             preferred_element_type=F32)

    def weights():
        carry = carry_scr[...]
        st["a"] = []
        for r in range(pages):
            excl = st["excl"][r * rows:(r + 1) * rows]
            st["a"].append(_sb_weights(st["own"][r], excl, carry, None).astype(BF16))
            carry = carry + excl[:, :1] + st["sp"][r][:, :1]
        carry_scr[...] = carry

    def values():
        acc = jnp.zeros(acc_scr.shape, F32)
        for r in range(pages):
            acc = acc + lax.dot_general(st["a"][r], v_pages[r][...].astype(BF16), _NT,
                                        preferred_element_type=F32)
        acc_scr[...] += acc

    return [logits, softplus, suffix, weights, values]


def _sample_end(t_new, q_ref, bias_ref, o_ref, acc_scr):
    rows = q_ref.shape[0]
    _, _, head_mask = _sample_queries(q_ref, bias_ref, t_new)
    accm = jnp.where(head_mask, acc_scr[...], 0.0)
    out = accm[0:t_new, :]
    for h in range(1, rows // t_new):
        out = out + accm[h * t_new:(h + 1) * t_new, :]
    o_ref[...] = out.astype(o_ref.dtype)


def _page_copies(page_ref, step, slot, ck_ref, cv_ref, kbuf, vbuf, sem):
    pages = kbuf.shape[1]
    copies = []
    for r in range(pages):
        pid = page_ref[step * pages + r]
        copies.append(pltpu.make_async_copy(ck_ref.at[pid], kbuf.at[slot, r], sem.at[0, slot]))
        copies.append(pltpu.make_async_copy(cv_ref.at[pid], vbuf.at[slot, r], sem.at[1, slot]))
    return copies


def _attn_kernel(qi_ref, kj_ref, seq_ref, page_ref, bias_ref, q_ref, k_ref, v_ref, ls_ref,
                 qs_ref, kn_ref, vn_ref, bcol_ref, lsn_ref, lsp_ref, ck_ref, cv_ref,
                 o_ref, os_ref, carry_scr, acc_scr, carry_s, acc_s, kbuf, vbuf, sem,
                 *, t_new, seq_steps, sample_steps):
    pages = kbuf.shape[1]
    n = pl.program_id(1)
    lin = pl.program_id(0) * pl.num_programs(1) + n
    active = lin < sample_steps
    s = lax.rem(lin, seq_steps)
    slot = lax.rem(lin, 2)

    def copies(step, into):
        return _page_copies(page_ref, step, into, ck_ref, cv_ref, kbuf, vbuf, sem)

    @pl.when(lin == 0)
    def _():
        for c in copies(0, 0):
            c.start()

    @pl.when(lin + 1 < sample_steps)
    def _():
        for c in copies(lin + 1, 1 - slot):
            c.start()

    @pl.when(active & (s == 0))
    def _():
        _sample_begin(seq_ref[lin], t_new, qs_ref, kn_ref, vn_ref, bcol_ref, lsn_ref,
                      carry_s, acc_s)

    _prompt_step(qi_ref[n], kj_ref[n], bias_ref, q_ref, k_ref, v_ref, ls_ref, o_ref,
                 carry_scr, acc_scr)

    @pl.when(active)
    def _():
        for c in copies(lin, slot):
            c.wait()
        k_pages = [kbuf.at[slot, r] for r in range(pages)]
        v_pages = [vbuf.at[slot, r] for r in range(pages)]
        group = min(pages, SAMPLE_GROUP)
        chains = [_sample_page_stages(t_new, qs_ref, bcol_ref, lsp_ref, k_pages[g:g + group],
                                      v_pages[g:g + group], carry_s, acc_s)
                  for g in range(0, pages, group)]
        n_stage = len(chains[0])
        for t in range(len(chains) + n_stage - 1):
            for st in reversed(range(n_stage)):
                if 0 <= t - st < len(chains):
                    chains[t - st][st]()

    @pl.when(active & (s == seq_steps - 1))
    def _():
        _sample_end(t_new, qs_ref, bcol_ref, os_ref, acc_s)


def _attn_call(l, q, k, v, bias, batch, seq, q_rep, k_new, v_new, bias_col, cache_k, cache_v,
               table):
    d_b = q.shape[1]
    assert ATT_TQ == ATT_TK and seq % ATT_TQ == 0 and ATT_TK % ATT_SUB == 0
    nq = seq // ATT_TQ
    qi = jnp.asarray([i for i in range(nq) for _ in range(i + 1)], jnp.int32)
    kj = jnp.asarray([j for i in range(nq) for j in range(i, -1, -1)], jnp.int32)
    n_steps = qi.shape[0]

    db, rows, _ = q_rep.shape
    t_new = rows // H_B
    n_new = k_new.shape[0]
    page = cache_k.shape[2]
    n_pages = table.shape[0] // db
    pages = min(p for p in range(1, n_pages + 1)
                if n_pages % p == 0 and db * (n_pages // p) <= batch * n_steps)
    seq_steps = n_pages // pages
    sample_steps = db * seq_steps

    lin = jnp.minimum(jnp.arange(batch * n_steps, dtype=jnp.int32), sample_steps - 1)
    seq_of = lin // seq_steps
    logical = n_pages - 1 - ((lin % seq_steps)[:, None] * pages
                             + jnp.arange(pages, dtype=jnp.int32)[None, :])
    page_of = table.reshape(db, n_pages)[seq_of[:, None], logical].reshape(-1)

    def per_seq(shape):
        return pl.BlockSpec((None,) + shape,
                            lambda b, n, qi, kj, sq, pg: (sq[b * n_steps + n], 0, 0))

    const = lambda shape: pl.BlockSpec(shape, lambda b, n, qi, kj, sq, pg: (0,) * len(shape))
    grid_spec = pltpu.PrefetchScalarGridSpec(
        num_scalar_prefetch=4,
        grid=(batch, n_steps),
        in_specs=[
            pl.BlockSpec(memory_space=pltpu.SMEM),
            pl.BlockSpec((ATT_TQ, d_b), lambda b, n, qi, kj, sq, pg: (b * nq + qi[n], 0)),
            pl.BlockSpec((None, None, d_b, ATT_TK), lambda b, n, qi, kj, sq, pg: (l, b, 0, kj[n])),
            pl.BlockSpec((None, None, d_b, ATT_TK), lambda b, n, qi, kj, sq, pg: (l, b, 0, kj[n])),
            const((ATT_SUB, ATT_SUB)),
            per_seq((rows, d_b)),
            const((n_new, d_b)), const((n_new, d_b)),
            const((rows, 1)), const((n_new, n_new)), const((page, page)),
            pl.BlockSpec(memory_space=pl.ANY), pl.BlockSpec(memory_space=pl.ANY),
        ],
        out_specs=(
            pl.BlockSpec((ATT_TQ, d_b), lambda b, n, qi, kj, sq, pg: (b * nq + qi[n], 0)),
            per_seq((t_new, d_b)),
        ),
        scratch_shapes=[
            pltpu.VMEM((H_B, ATT_TQ, 1), F32),
            pltpu.VMEM((ATT_TQ, d_b), F32),
            pltpu.VMEM((rows, 1), F32),
            pltpu.VMEM((rows, d_b), F32),
            pltpu.VMEM((2, pages, d_b, page), F32),
            pltpu.VMEM((2, pages, d_b, page), F32),
            pltpu.SemaphoreType.DMA((2, 2)),
        ],
    )
    return pl.pallas_call(
        functools.partial(_attn_kernel, t_new=t_new, seq_steps=seq_steps,
                          sample_steps=sample_steps),
        grid_spec=grid_spec,
        out_shape=(jax.ShapeDtypeStruct(q.shape, BF16),
                   jax.ShapeDtypeStruct((db, t_new, d_b), BF16)),
        compiler_params=pltpu.CompilerParams(
            dimension_semantics=("arbitrary", "arbitrary"), vmem_limit_bytes=VMEM_LIMIT),
    )(qi, kj, seq_of, page_of, bias, q, k, v, _strict_lower(ATT_SUB),
      q_rep, k_new, v_new, bias_col, _strict_lower(n_new), _strict_lower(page),
      cache_k, cache_v)


def _post_kernel(x_ref, o_ref, u_ref, vn_ref, ga_ref, gb_ref, g1_ref, mix_ref, bcol_ref,
                 wa_ref, wb_ref, wo_ref, lng_ref, lnb_ref, y_ref, ya_scr, *, alpha):
    tm = x_ref.shape[0]
    r = lax.broadcasted_iota(jnp.int32, (CHUNK, CHUNK), 0)
    c = lax.broadcasted_iota(jnp.int32, (CHUNK, CHUNK), 1)
    causal = r >= c
    cg = vn_ref.shape[1] // G_A
    for g in range(G_A):
        w = jnp.where(causal, mix_ref[g], 0.0).astype(BF16)
        bias = bcol_ref[:, g:g + 1]
        cols = slice(g * cg, (g + 1) * cg)
        for ch in range(tm // CHUNK):
            rows = slice(ch * CHUNK, (ch + 1) * CHUNK)
            mixed = jnp.dot(w, vn_ref[rows, cols].astype(BF16), preferred_element_type=F32) + bias
            ya_scr[rows, cols] = (u_ref[rows, cols].astype(F32) * mixed).astype(BF16)
    y_a = jnp.dot(ya_scr[...], wa_ref[...], preferred_element_type=F32)
    y_b = jnp.dot(o_ref[...], wb_ref[...], preferred_element_type=F32)
    merged = ga_ref[...].astype(F32) * y_a + gb_ref[...].astype(F32) * y_b
    t = jnp.dot(merged.astype(BF16), wo_ref[...], preferred_element_type=F32)
    y_ref[...] = _ln(alpha * x_ref[...] + g1_ref[...] * t, lng_ref[...], lnb_ref[...])


def _post_call(l, x2, o, u, vn, ga, gb, mods, per_row, tiles_per_batch, tm, mix, bcol,
               w_a, w_b, w_o, ln_g, ln_b, alpha):
    n, d = x2.shape
    d_a, d_b = u.shape[1], o.shape[1]
    row = lambda c: pl.BlockSpec((tm, c), lambda i: (i, 0))
    return pl.pallas_call(
        functools.partial(_post_kernel, alpha=alpha),
        grid=(n // tm,),
        in_specs=[
            row(d), row(d_b), row(d_a), row(d_a), row(d), row(d),
            _mod_spec(l, per_row, tm, d, tiles_per_batch, 2),
            _layer_spec(l, mix.shape[1:]), _layer_spec(l, bcol.shape[1:]),
            _layer_spec(l, w_a.shape[1:]), _layer_spec(l, w_b.shape[1:]),
            _layer_spec(l, w_o.shape[1:]),
            _layer_spec(l, (1, d)), _layer_spec(l, (1, d)),
        ],
        out_specs=row(d),
        out_shape=jax.ShapeDtypeStruct((n, d), F32),
        scratch_shapes=[pltpu.VMEM((tm, d_a), BF16)],
        compiler_params=pltpu.CompilerParams(
            dimension_semantics=("arbitrary",), vmem_limit_bytes=VMEM_LIMIT),
    )(x2, o, u, vn, ga, gb, mods, mix, bcol, w_a, w_b, w_o, ln_g, ln_b)


def _mlp_kernel(x_ref, sh_ref, sc_ref, g2_ref, wu_ref, bu_ref, wd_ref, bd_ref, lng_ref, lnb_ref,
                y_ref, *, alpha, ff_chunk):
    x = x_ref[...]
    h = (x * (1.0 + sc_ref[...]) + sh_ref[...]).astype(BF16)
    d_ff = wu_ref.shape[1]
    f = jnp.zeros(x.shape, F32)
    for c in range(d_ff // ff_chunk):
        cols = slice(c * ff_chunk, (c + 1) * ff_chunk)
        hid = jnp.dot(h, wu_ref[:, cols], preferred_element_type=F32) + bu_ref[:, cols]
        hid = jnp.maximum(hid, 0.0)
        f = f + jnp.dot((hid * hid).astype(BF16), wd_ref[cols, :], preferred_element_type=F32)
    f = f + bd_ref[...]
    y_ref[...] = _ln(alpha * x + g2_ref[...] * f, lng_ref[...], lnb_ref[...])


def _mlp_call(l, x2, mods, per_row, tiles_per_batch, tm, w_up, b_up, w_down, b_down, ln_g, ln_b,
              alpha):
    n, d = x2.shape
    d_ff = w_up.shape[2]
    row = pl.BlockSpec((tm, d), lambda i: (i, 0))
    return pl.pallas_call(
        functools.partial(_mlp_kernel, alpha=alpha, ff_chunk=1024),
        grid=(n // tm,),
        in_specs=[
            row,
            _mod_spec(l, per_row, tm, d, tiles_per_batch, 3),
            _mod_spec(l, per_row, tm, d, tiles_per_batch, 4),
            _mod_spec(l, per_row, tm, d, tiles_per_batch, 5),
            _layer_spec(l, (d, d_ff)), _layer_spec(l, (1, d_ff)),
            _layer_spec(l, (d_ff, d)), _layer_spec(l, (1, d)),
            _layer_spec(l, (1, d)), _layer_spec(l, (1, d)),
        ],
        out_specs=row,
        out_shape=jax.ShapeDtypeStruct((n, d), F32),
        compiler_params=pltpu.CompilerParams(
            dimension_semantics=("arbitrary",), vmem_limit_bytes=VMEM_LIMIT),
    )(x2, mods, mods, mods, w_up, b_up, w_down, b_down, ln_g, ln_b)


def kernel(x_prompt, x_sample, cache_k, cache_v, page_table, c_prompt, c_sample, w_ada, b_ada, w_in, b_sb, ln_v_g, ln_v_b, w_s, b_s, w_a, w_b, w_o, ln1_g, ln1_b, w_up, b_up, w_down, b_down, ln2_g, ln2_b):
    batch, seq, d = x_prompt.shape
    db, t_new, _ = x_sample.shape
    depth, n_pool, page, n_heads, hd = cache_k.shape
    assert (n_heads, hd) == (H_B, HD_B) and page == CHUNK
    d_a = ln_v_g.shape[1]
    d_b = n_heads * hd
    n_pages = page_table.shape[1]
    alpha = (2 * depth) ** 0.25
    n_s = db * t_new
    assert CHUNK % t_new == 0 and n_s % CHUNK == 0

    n_c = batch + db
    pad = (-n_c) % 8
    c_all = jnp.concatenate([c_prompt, c_sample, jnp.zeros((pad, d), F32)], axis=0)
    mods = _mods_call(c_all, w_ada, b_ada)

    cache_k2 = jnp.transpose(cache_k, (0, 1, 3, 4, 2)).reshape(depth * n_pool, d_b, page)
    cache_v2 = jnp.transpose(cache_v, (0, 1, 3, 4, 2)).reshape(depth * n_pool, d_b, page)
    table = page_table.reshape(-1).astype(jnp.int32)

    tm_p = 1024
    tm_s = CHUNK
    tpb = seq // tm_p
    n_open = CHUNK // t_new

    w_in16, w_a16, w_b16, w_o16 = (w.astype(BF16) for w in (w_in, w_a, w_b, w_o))
    w_up16, w_down16 = w_up.astype(BF16), w_down.astype(BF16)
    row3 = lambda p: p.reshape(depth, 1, p.shape[1])
    lnv_g, lnv_b, ln1g, ln1b, ln2g, ln2b = map(row3, (ln_v_g, ln_v_b, ln1_g, ln1_b, ln2_g, ln2_b))
    b_up3, b_down3 = row3(b_up), row3(b_down)
    mods_p = mods[:, :batch].reshape(depth, batch, 6, 1, d)
    mods_s = jnp.repeat(mods[:, batch:n_c], t_new, axis=1)
    bcol_p = jnp.swapaxes(b_s, 1, 2)
    pos = jnp.arange(CHUNK, dtype=jnp.int32) // t_new
    mix_s = jnp.where(pos[:, None] == pos[None, :],
                      jnp.tile(w_s[:, :, :t_new, :t_new], (1, 1, n_open, n_open)), 0.0)
    bcol_s = jnp.tile(bcol_p[:, :t_new], (1, n_open, 1))
    bias_col = jnp.repeat(b_sb, t_new, axis=1).reshape(depth, n_heads * t_new, 1)

    yp = x_prompt.reshape(batch * seq, d)
    ys = x_sample.reshape(n_s, d)
    kv_p = ()
    gv_p, k_s, v_s, gv_s = [], [], [], []
    for l in range(depth):
        u_p, vn_p, q_p, *kv_p, ga_p, gb_p = _inproj_call(
            l, yp, mods_p, False, tpb, tm_p, w_in16, lnv_g, lnv_b, d_a, d_b, tuple(kv_p))
        u_s, vn_s, q_s, k_sl, v_sl, ga_s, gb_s = _inproj_call(
            l, ys, mods_s, True, 1, tm_s, w_in16, lnv_g, lnv_b, d_a, d_b, None)

        q_rep = jnp.tile(q_s.reshape(db, t_new, d_b), (1, n_heads, 1))
        o_p, o_s = _attn_call(l, q_p, kv_p[0], kv_p[1], b_sb[l], batch, seq, q_rep, k_sl, v_sl,
                              bias_col[l], cache_k2, cache_v2, table + l * n_pool)

        x1 = _post_call(l, yp, o_p, u_p, vn_p, ga_p, gb_p, mods_p, False, tpb, tm_p, w_s, bcol_p,
                        w_a16, w_b16, w_o16, ln1g, ln1b, alpha)
        yp = _mlp_call(l, x1, mods_p, False, tpb, tm_p, w_up16, b_up3, w_down16, b_down3,
                       ln2g, ln2b, alpha)
        x1 = _post_call(l, ys, o_s.reshape(n_s, d_b), u_s, vn_s, ga_s, gb_s, mods_s, True, 1, tm_s,
                        mix_s, bcol_s, w_a16, w_b16, w_o16, ln1g, ln1b, alpha)
        ys = _mlp_call(l, x1, mods_s, True, 1, tm_s, w_up16, b_up3, w_down16, b_down3,
                       ln2g, ln2b, alpha)

        gv_p.append(vn_p.reshape(batch, seq, d_a)[:, seq - CHUNK:])
        k_s.append(k_sl.reshape(db, t_new, n_heads, hd))
        v_s.append(v_sl.reshape(db, t_new, n_heads, hd))
        gv_s.append(vn_s.reshape(db, t_new, d_a))

    to_rows = lambda x: jnp.transpose(x.reshape(depth, batch, n_heads, hd, seq), (0, 1, 4, 2, 3))
    return (yp.reshape(batch, seq, d), ys.reshape(db, t_new, d),
            to_rows(kv_p[0]), to_rows(kv_p[1]), jnp.stack(gv_p),
            jnp.stack(k_s), jnp.stack(v_s), jnp.stack(gv_s))
```
